```python
import math
import jax, jax.numpy as jnp
from jax import lax
import numpy as np

D_MODEL = 1024
BATCH = 8
SEQ = 2048
DEPTH = 2

PLE_DIM = 256
N_MIXERS = 2
N_EVEN = (DEPTH + 1) // 2
N_ODD = DEPTH // 2

DH = 64
N_HEADS = D_MODEL // (2 * DH)
QK_WIDTH = N_HEADS * 2 * DH
V_WIDTH = N_HEADS * 2 * DH
QKV_WIDTH = 2 * QK_WIDTH + V_WIDTH
ATTN_SCALE = 1.0 / math.sqrt(DH)
ROPE_THETA = 10000.0
Q_BLOCK = 128

POOL_WINDOWS = (2, 4, 8, 16)
N_GROUPS = len(POOL_WINDOWS)
GROUP_W = D_MODEL // N_GROUPS

D_FF = 2816
N_EXPERTS = 8
TOP_K = 2
D_FF_EXPERT = 3584

RMS_EPS = 1e-6

kernel_name = "hybrid_diffattn_pool_moe_ple"


def rms_norm(x, g, eps=RMS_EPS):
    xf = x.astype(jnp.float32)
    y = xf * lax.rsqrt(jnp.mean(xf * xf, axis=-1, keepdims=True) + eps)
    return (y * g.astype(jnp.float32)).astype(x.dtype)


def rope_tables(seq, dim):
    inv = 1.0 / (ROPE_THETA ** (jnp.arange(0, dim, 2, dtype=jnp.float32) / dim))
    ang = jnp.arange(seq, dtype=jnp.float32)[:, None] * inv[None, :]
    return jnp.cos(ang), jnp.sin(ang)


def apply_rope(x, cos, sin):
    c = cos[None, :, None, None, :].astype(x.dtype)
    s = sin[None, :, None, None, :].astype(x.dtype)
    x1, x2 = jnp.split(x, 2, axis=-1)
    return jnp.concatenate([x1 * c - x2 * s, x2 * c + x1 * s], axis=-1)


def diff_attention(xn, w_qkv, w_o, q_norm, k_norm, lq1, lk1, lq2, lk2, subln_g, lam_init):
    B, S, _ = xn.shape
    qkv = xn @ w_qkv
    q, k, v = jnp.split(qkv, [QK_WIDTH, 2 * QK_WIDTH], axis=-1)
    q = q.reshape(B, S, N_HEADS, 2, DH)
    k = k.reshape(B, S, N_HEADS, 2, DH)
    v = v.reshape(B, S, N_HEADS, 2 * DH)
    cos, sin = rope_tables(S, DH)
    q = apply_rope(rms_norm(q, q_norm), cos, sin)
    k = apply_rope(rms_norm(k, k_norm), cos, sin)
    lam = (jnp.exp(jnp.sum(lq1.astype(jnp.float32) * lk1.astype(jnp.float32)))
           - jnp.exp(jnp.sum(lq2.astype(jnp.float32) * lk2.astype(jnp.float32)))
           + lam_init)
    outs = []
    for blk in range(S // Q_BLOCK):
        q0 = blk * Q_BLOCK
        q1 = q0 + Q_BLOCK
        qb = q[:, q0:q1]
        kb = k[:, :q1]
        vb = v[:, :q1]
        s = jnp.einsum('bqhcd,bkhcd->bhcqk', qb, kb).astype(jnp.float32) * ATTN_SCALE
        mask = jnp.arange(q1)[None, :] <= jnp.arange(q0, q1)[:, None]
        s = jnp.where(mask, s, -jnp.inf)
        a = jax.nn.softmax(s, axis=-1)
        a = a[:, :, 0] - lam * a[:, :, 1]
        outs.append(jnp.einsum('bhqk,bkhe->bqhe', a.astype(vb.dtype), vb))
    o = jnp.concatenate(outs, axis=1)
    o = rms_norm(o, subln_g) * (1.0 - lam_init)
    return o.reshape(B, S, V_WIDTH) @ w_o


def pool_mixer(xn, w_pool, b_pool, scale):
    B, S, D = xn.shape
    xg = xn.reshape(B, S, N_GROUPS, GROUP_W).astype(jnp.float32)
    csum = jnp.cumsum(xg, axis=1)
    t = jnp.arange(S)
    outs = []
    for g, w in enumerate(POOL_WINDOWS):
        cg = csum[:, :, g]
        prev = jnp.pad(cg, ((0, 0), (w, 0), (0, 0)))[:, :S]
        cnt = jnp.minimum(t + 1, w).astype(jnp.float32)[None, :, None]
        outs.append((cg - prev) / cnt - xg[:, :, g])
    y = jnp.stack(outs, axis=2).astype(xn.dtype)
    y = jnp.einsum('bsgc,gce->bsge', y, w_pool) + b_pool
    return y.reshape(B, S, D) * scale


def swiglu(xn, w_gate, w_up, w_down):
    return (jax.nn.silu(xn @ w_gate) * (xn @ w_up)) @ w_down


def moe_swiglu(xn, w_router, w_gate, w_up, w_down):
    logits = (xn @ w_router).astype(jnp.float32)
    top_v, top_i = lax.top_k(logits, TOP_K)
    top_w = jax.nn.softmax(top_v, axis=-1)
    gates = jnp.sum(jax.nn.one_hot(top_i, N_EXPERTS, dtype=jnp.float32) * top_w[..., None], axis=-2)
    out = jnp.zeros_like(xn)
    for e in range(N_EXPERTS):
        ye = swiglu(xn, w_gate[e], w_up[e], w_down[e])
        out = out + gates[..., e:e + 1].astype(xn.dtype) * ye
    return out


def setup_inputs(seed: int = 0) -> dict:
    key = jax.random.key(seed)
    ks = iter(jax.random.split(key, 40))
    f32 = jnp.float32

    def w(shape, fan_in):
        return jax.random.normal(next(ks), shape, f32) * (fan_in ** -0.5)

    def gain(shape):
        return 1.0 + 0.02 * jax.random.normal(next(ks), shape, f32)

    return {
        "x": jax.random.normal(next(ks), (BATCH, SEQ, D_MODEL), f32),
        "p": jax.random.normal(next(ks), (DEPTH, BATCH, SEQ, PLE_DIM), f32),
        "norm_mix": gain((DEPTH, D_MODEL)),
        "norm_ffn": gain((DEPTH, D_MODEL)),
        "norm_ple": gain((DEPTH, D_MODEL)),
        "attn_w_qkv": w((N_EVEN, D_MODEL, QKV_WIDTH), D_MODEL),
        "attn_w_o": w((N_EVEN, V_WIDTH, D_MODEL), V_WIDTH),
        "attn_q_norm": gain((N_EVEN, DH)),
        "attn_k_norm": gain((N_EVEN, DH)),
        "attn_lambda_q1": 0.1 * jax.random.normal(next(ks), (N_EVEN, DH), f32),
        "attn_lambda_k1": 0.1 * jax.random.normal(next(ks), (N_EVEN, DH), f32),
        "attn_lambda_q2": 0.1 * jax.random.normal(next(ks), (N_EVEN, DH), f32),
        "attn_lambda_k2": 0.1 * jax.random.normal(next(ks), (N_EVEN, DH), f32),
        "attn_subln": gain((N_EVEN, 2 * DH)),
        "pool_w": w((N_ODD, N_GROUPS, GROUP_W, GROUP_W), GROUP_W),
        "pool_b": 0.02 * jax.random.normal(next(ks), (N_ODD, N_GROUPS, GROUP_W), f32),
        "pool_scale": 1.0 + 0.1 * jax.random.normal(next(ks), (N_ODD, D_MODEL), f32),
        "ffn_w_gate": w((N_EVEN, D_MODEL, D_FF), D_MODEL),
        "ffn_w_up": w((N_EVEN, D_MODEL, D_FF), D_MODEL),
        "ffn_w_down": w((N_EVEN, D_FF, D_MODEL), D_FF),
        "moe_router": w((N_ODD, D_MODEL, N_EXPERTS), D_MODEL),
        "moe_w_gate": w((N_ODD, N_EXPERTS, D_MODEL, D_FF_EXPERT), D_MODEL),
        "moe_w_up": w((N_ODD, N_EXPERTS, D_MODEL, D_FF_EXPERT), D_MODEL),
        "moe_w_down": w((N_ODD, N_EXPERTS, D_FF_EXPERT, D_MODEL), D_FF_EXPERT),
        "ple_w_proj": w((DEPTH, PLE_DIM, D_MODEL), PLE_DIM),
        "ple_w_gate": w((DEPTH, D_MODEL, D_MODEL), D_MODEL),
    }


def reference(x, p, norm_mix, norm_ffn, norm_ple,
              attn_w_qkv, attn_w_o, attn_q_norm, attn_k_norm,
              attn_lambda_q1, attn_lambda_k1, attn_lambda_q2, attn_lambda_k2, attn_subln,
              pool_w, pool_b, pool_scale,
              ffn_w_gate, ffn_w_up, ffn_w_down,
              moe_router, moe_w_gate, moe_w_up, moe_w_down,
              ple_w_proj, ple_w_gate):
    h = x
    for i in range(DEPTH):
        j = i // 2
        xn = rms_norm(h, norm_mix[i])
        if i % N_MIXERS == 0:
            lam_init = 0.8 - 0.6 * math.exp(-0.3 * i)
            mix = diff_attention(xn, attn_w_qkv[j], attn_w_o[j], attn_q_norm[j], attn_k_norm[j],
                                 attn_lambda_q1[j], attn_lambda_k1[j],
                                 attn_lambda_q2[j], attn_lambda_k2[j], attn_subln[j], lam_init)
        else:
            mix = pool_mixer(xn, pool_w[j], pool_b[j], pool_scale[j])
        h = h + mix
        xn = rms_norm(h, norm_ffn[i])
        if i % 2 == 0:
            ff = swiglu(xn, ffn_w_gate[j], ffn_w_up[j], ffn_w_down[j])
        else:
            ff = moe_swiglu(xn, moe_router[j], moe_w_gate[j], moe_w_up[j], moe_w_down[j])
        h = h + ff
        gate = jax.nn.sigmoid(rms_norm(h, norm_ple[i]) @ ple_w_gate[i])
        h = h + gate * (p[i].astype(h.dtype) @ ple_w_proj[i])
    return h
```

```python
import functools
import math

import jax
import jax.numpy as jnp
from jax import lax
from jax.experimental import pallas as pl
from jax.experimental.pallas import tpu as pltpu

F32 = jnp.float32
BF16 = jnp.bfloat16

DH = 64
HEAD_W = 2 * DH
ATTN_SCALE = 1.0 / math.sqrt(DH)
ROPE_THETA = 10000.0
POOL_WINDOWS = (2, 4, 8, 16)
POOL_HALO = 16
TOP_K = 2
RMS_EPS = 1e-6
LANES = 128
VMEM_LIMIT = 56 * 1024 * 1024

ROW_TILE = 512
Q_TILE = 256
EXPERT_TILE = 512
EXPERT_F_TILE = 1792
GATHER_ROWS = 2048


def _rms(x, gain):
    return x * lax.rsqrt(jnp.mean(x * x, axis=-1, keepdims=True) + RMS_EPS) * gain


def _const_spec(shape):
    zeros = (0,) * len(shape)
    return pl.BlockSpec(shape, lambda *_: zeros, pipeline_mode=pl.Buffered(1))


def _params(*sem):
    return pltpu.CompilerParams(dimension_semantics=sem, vmem_limit_bytes=VMEM_LIMIT)


def _qkv_kernel(x_ref, g_ref, w_ref, cos_ref, sin_ref, qkg_ref, ones_ref, o_ref, xn_ref):
    j = pl.program_id(1)

    @pl.when(j == 0)
    def _():
        xn_ref[...] = _rms(x_ref[...], g_ref[...]).astype(BF16)

    y = jnp.dot(xn_ref[...], w_ref[...], preferred_element_type=F32)

    @pl.when(j == 2)
    def _():
        o_ref[...] = y.astype(BF16)

    @pl.when(j < 2)
    def _():
        gain = qkg_ref[pl.ds(j, 1), :]
        cos = cos_ref[...]
        sin = sin_ref[...]
        lane = lax.broadcasted_iota(jnp.int32, cos.shape, 1)
        first_half = (lane % DH) < (DH // 2)
        ones = ones_ref[...]
        width = ones.shape[0]
        for c in range(0, y.shape[1], width):
            yc = y[:, c:c + width]
            yy = yc * yc
            hi = yy.astype(BF16)
            lo = (yy - hi.astype(F32)).astype(BF16)
            ss = (jnp.dot(hi, ones, preferred_element_type=F32)
                  + jnp.dot(lo, ones, preferred_element_type=F32))
            yn = yc * lax.rsqrt(ss * (1.0 / DH) + RMS_EPS)
            for s in range(0, width, LANES):
                z = yn[:, s:s + LANES] * gain
                rot = jnp.where(first_half,
                                pltpu.roll(z, LANES - DH // 2, 1),
                                pltpu.roll(z, DH // 2, 1))
                o_ref[:, c + s:c + s + LANES] = (z * cos + rot * sin).astype(BF16)


def _qkv(x2, gain, w_qkv, cos_t, sin_t, qk_gain, ones):
    T, D = x2.shape
    S = cos_t.shape[0]
    tm = ROW_TILE
    blocks_per_seq = S // tm
    return pl.pallas_call(
        _qkv_kernel,
        grid=(T // tm, 3),
        in_specs=[
            pl.BlockSpec((tm, D), lambda i, j: (i, 0)),
            _const_spec((1, D)),
            pl.BlockSpec((D, D), lambda i, j: (0, j)),
            pl.BlockSpec((tm, LANES), lambda i, j: (i % blocks_per_seq, 0)),
            pl.BlockSpec((tm, LANES), lambda i, j: (i % blocks_per_seq, 0)),
            _const_spec(qk_gain.shape),
            _const_spec(ones.shape),
        ],
        out_specs=pl.BlockSpec((tm, D), lambda i, j: (i, j)),
        out_shape=jax.ShapeDtypeStruct((T, 3 * D), BF16),
        scratch_shapes=[pltpu.VMEM((tm, D), BF16)],
        compiler_params=_params("parallel", "arbitrary"),
        name="qkv_rope",
    )(x2, gain, w_qkv, cos_t, sin_t, qk_gain, ones)


def _dot_nt(a, b):
    return lax.dot_general(a, b, (((1,), (1,)), ((), ())), preferred_element_type=F32)


def _attn_kernel(q_ref, k_ref, v_ref, lam_ref, g_ref, o_ref, *, lam_init):
    lp = lam_ref[...]
    lam = (jnp.exp(jnp.sum(lp[0:1] * lp[1:2], axis=-1, keepdims=True))
           - jnp.exp(jnp.sum(lp[2:3] * lp[3:4], axis=-1, keepdims=True)) + lam_init)
    S = q_ref.shape[1]
    tq = Q_TILE
    lane = lax.broadcasted_iota(jnp.int32, (tq, HEAD_W), 1)
    row = lax.broadcasted_iota(jnp.int32, (tq, tq), 0)
    col = lax.broadcasted_iota(jnp.int32, (tq, tq), 1)
    future = col > row
    gain = g_ref[...]
    for i in range(S // tq):
        q = q_ref[0, i * tq:(i + 1) * tq, :]
        qs = (jnp.where(lane < DH, q, jnp.zeros_like(q)), jnp.where(lane >= DH, q, jnp.zeros_like(q)))
        k_diag = k_ref[0, i * tq:(i + 1) * tq, :]
        probs = []
        for qc in qs:
            s_diag = jnp.where(future, -jnp.inf, _dot_nt(qc, k_diag))
            m = jnp.max(s_diag, axis=-1, keepdims=True)
            if i > 0:
                s_off = _dot_nt(qc, k_ref[0, :i * tq, :])
                m = jnp.maximum(m, jnp.max(s_off, axis=-1, keepdims=True))
                p_off = jnp.exp(s_off - m)
            p_diag = jnp.exp(s_diag - m)
            denom = jnp.sum(p_diag, axis=-1, keepdims=True)
            if i > 0:
                denom = denom + jnp.sum(p_off, axis=-1, keepdims=True)
                probs.append((p_diag, p_off, denom))
            else:
                probs.append((p_diag, None, denom))
        r0 = 1.0 / probs[0][2]
        r1 = lam / probs[1][2]
        a_diag = probs[0][0] * r0 - probs[1][0] * r1
        o = jnp.dot(a_diag.astype(BF16), v_ref[0, i * tq:(i + 1) * tq, :], preferred_element_type=F32)
        if i > 0:
            a_off = probs[0][1] * r0 - probs[1][1] * r1
            o = o + jnp.dot(a_off.astype(BF16), v_ref[0, :i * tq, :], preferred_element_type=F32)
        o = _rms(o, gain) * (1.0 - lam_init)
        o_ref[0, i * tq:(i + 1) * tq, :] = o.astype(BF16)


def _attention(qkv3, lam_params, subln, lam_init):
    B, S, W3 = qkv3.shape
    H = W3 // 3 // HEAD_W
    blk = (1, S, HEAD_W)
    return pl.pallas_call(
        functools.partial(_attn_kernel, lam_init=lam_init),
        grid=(B, H),
        in_specs=[
            pl.BlockSpec(blk, lambda b, h: (b, 0, h)),
            pl.BlockSpec(blk, lambda b, h: (b, 0, H + h)),
            pl.BlockSpec(blk, lambda b, h: (b, 0, 2 * H + h)),
            _const_spec(lam_params.shape),
            _const_spec(subln.shape),
        ],
        out_specs=pl.BlockSpec(blk, lambda b, h: (b, 0, h)),
        out_shape=jax.ShapeDtypeStruct((B, S, H * HEAD_W), BF16),
        compiler_params=_params("parallel", "parallel"),
        name="diff_attn",
    )(qkv3, qkv3, qkv3, lam_params, subln)


def _ple(h, p, g_ple, w_gate, w_proj):
    xn = _rms(h, g_ple).astype(BF16)
    gate = jax.nn.sigmoid(jnp.dot(xn, w_gate, preferred_element_type=F32))
    proj = jnp.dot(p.astype(BF16), w_proj, preferred_element_type=F32)
    return h + gate * proj


def _dense_tail_kernel(ao_ref, x_ref, p_ref, wo_ref, gf_ref, wg_ref, wu_ref, wd_ref,
                       gp_ref, wpg_ref, wpp_ref, o_ref, *, f_chunks):
    h = x_ref[...] + jnp.dot(ao_ref[...], wo_ref[...], preferred_element_type=F32)
    xn = _rms(h, gf_ref[...]).astype(BF16)
    acc = h
    for lo, hi in f_chunks:
        g = jnp.dot(xn, wg_ref[:, lo:hi], preferred_element_type=F32)
        u = jnp.dot(xn, wu_ref[:, lo:hi], preferred_element_type=F32)
        act = (g * jax.nn.sigmoid(g) * u).astype(BF16)
        acc = acc + jnp.dot(act, wd_ref[lo:hi, :], preferred_element_type=F32)
    o_ref[...] = _ple(acc, p_ref[...], gp_ref[...], wpg_ref[...], wpp_ref[...])


def _dense_tail(ao, x2, p2, w_o, g_ffn, w_gate, w_up, w_down, g_ple, w_pg, w_pp):
    T, D = x2.shape
    F = w_gate.shape[1]
    PD = p2.shape[1]
    tm = ROW_TILE
    f_chunks = tuple((lo, min(lo + 1024, F)) for lo in range(0, F, 1024))
    row = lambda w: pl.BlockSpec((tm, w), lambda i: (i, 0))
    return pl.pallas_call(
        functools.partial(_dense_tail_kernel, f_chunks=f_chunks),
        grid=(T // tm,),
        in_specs=[row(D), row(D), row(PD), _const_spec((D, D)), _const_spec((1, D)),
                  _const_spec((D, F)), _const_spec((D, F)), _const_spec((F, D)),
                  _const_spec((1, D)), _const_spec((D, D)), _const_spec((PD, D))],
        out_specs=row(D),
        out_shape=jax.ShapeDtypeStruct((T, D), F32),
        compiler_params=_params("parallel"),
        name="dense_tail",
    )(ao, x2, p2, w_o, g_ffn, w_gate, w_up, w_down, g_ple, w_pg, w_pp)


def _pool_router_kernel(h_ref, halo_ref, gm_ref, pw_ref, pb_ref, ps_ref, gf_ref, wr_ref,
                        h_out_ref, xn_out_ref, info_ref, cnt_ref, xbuf_ref, carry_ref,
                        *, blocks_per_seq, n_experts):
    i = pl.program_id(0)
    tm, D = h_ref.shape
    gw = D // len(POOL_WINDOWS)
    seq_block = i % blocks_per_seq

    @pl.when(i == 0)
    def _():
        carry_ref[...] = jnp.zeros_like(carry_ref)

    h = h_ref[...]
    gm = gm_ref[...]
    xbuf_ref[POOL_HALO:, :] = _rms(h, gm)

    @pl.when(seq_block == 0)
    def _():
        xbuf_ref[:POOL_HALO, :] = jnp.zeros((POOL_HALO, D), F32)

    @pl.when(seq_block != 0)
    def _():
        xbuf_ref[:POOL_HALO, :] = _rms(halo_ref[...], gm)

    t = seq_block * tm + lax.broadcasted_iota(jnp.int32, (tm, 1), 0)
    pieces = []
    for g, w in enumerate(POOL_WINDOWS):
        cs = slice(g * gw, (g + 1) * gw)
        win = xbuf_ref[POOL_HALO:, cs]
        tok = win
        for k in range(1, w):
            win = win + xbuf_ref[POOL_HALO - k:POOL_HALO - k + tm, cs]
        cnt = jnp.minimum(t + 1, w).astype(F32)
        y = (win / cnt - tok).astype(BF16)
        pieces.append(jnp.dot(y, pw_ref[g], preferred_element_type=F32))
    mix = (jnp.concatenate(pieces, axis=1) + pb_ref[...]) * ps_ref[...]
    h2 = h + mix
    h_out_ref[...] = h2
    xn = _rms(h2, gf_ref[...])
    xn_out_ref[...] = xn

    logits = jnp.dot(xn, wr_ref[...], preferred_element_type=F32, precision=lax.Precision.HIGHEST)
    lane = lax.broadcasted_iota(jnp.int32, logits.shape, 1)
    lg = jnp.where(lane < n_experts, logits, -jnp.inf)
    m1 = jnp.max(lg, axis=-1, keepdims=True)
    i1 = jnp.min(jnp.where(lg == m1, lane, LANES), axis=-1, keepdims=True)
    lg2 = jnp.where(lane == i1, -jnp.inf, lg)
    m2 = jnp.max(lg2, axis=-1, keepdims=True)
    i2 = jnp.min(jnp.where(lg2 == m2, lane, LANES), axis=-1, keepdims=True)
    e = jnp.exp(m2 - m1)
    w1 = 1.0 / (1.0 + e)
    w2 = e / (1.0 + e)

    sel1 = lane == i1
    sel2 = lane == i2
    onehot = jnp.where(sel1 | sel2, 1.0, 0.0)
    r = lax.broadcasted_iota(jnp.int32, (tm, tm), 0)
    c = lax.broadcasted_iota(jnp.int32, (tm, tm), 1)
    strict_lower = jnp.where(c < r, 1.0, 0.0).astype(BF16)
    before = jnp.dot(strict_lower, onehot.astype(BF16), preferred_element_type=F32) + carry_ref[0:1, :]
    rank1 = jnp.sum(jnp.where(sel1, before, 0.0), axis=-1, keepdims=True)
    rank2 = jnp.sum(jnp.where(sel2, before, 0.0), axis=-1, keepdims=True)
    total = carry_ref[0:1, :] + jnp.sum(onehot, axis=0, keepdims=True)
    carry_ref[...] = jnp.broadcast_to(total, carry_ref.shape)
    cnt_ref[...] = jnp.broadcast_to(total, cnt_ref.shape)

    info = jnp.where(lane == 0, i1.astype(F32), 0.0)
    info = jnp.where(lane == 1, i2.astype(F32), info)
    info = jnp.where(lane == 2, w1, info)
    info = jnp.where(lane == 3, w2, info)
    info = jnp.where(lane == 4, rank1, info)
    info = jnp.where(lane == 5, rank2, info)
    info_ref[...] = info


def _pool_router(h, S, g_mix, pool_w, pool_b, pool_scale, g_ffn, w_router_pad, n_experts):
    T, D = h.shape
    tm = ROW_TILE
    G, gw, _ = pool_w.shape
    halo_per_tile = tm // POOL_HALO
    row = pl.BlockSpec((tm, D), lambda i: (i, 0))
    return pl.pallas_call(
        functools.partial(_pool_router_kernel, blocks_per_seq=S // tm, n_experts=n_experts),
        grid=(T // tm,),
        in_specs=[
            row,
            pl.BlockSpec((POOL_HALO, D), lambda i: (jnp.maximum(i * halo_per_tile - 1, 0), 0)),
            _const_spec((1, D)), _const_spec((G, gw, gw)), _const_spec((1, D)), _const_spec((1, D)),
            _const_spec((1, D)), _const_spec((D, LANES)),
        ],
        out_specs=[row, row, pl.BlockSpec((tm, LANES), lambda i: (i, 0)),
                   pl.BlockSpec((8, LANES), lambda i: (0, 0))],
        out_shape=[jax.ShapeDtypeStruct((T, D), F32), jax.ShapeDtypeStruct((T, D), F32),
                   jax.ShapeDtypeStruct((T, LANES), F32), jax.ShapeDtypeStruct((8, LANES), F32)],
        scratch_shapes=[pltpu.VMEM((tm + POOL_HALO, D), F32), pltpu.VMEM((8, LANES), F32)],
        compiler_params=_params("arbitrary"),
        name="pool_router",
    )(h, h, g_mix, pool_w, pool_b, pool_scale, g_ffn, w_router_pad)


def _gather_kernel(idx_ref, src_ref, out_ref, sem):
    i = pl.program_id(0)
    n = idx_ref.shape[2]
    base = i * n

    def issue(r, carry):
        pltpu.make_async_copy(src_ref.at[pl.ds(idx_ref[0, 0, r], 1)],
                              out_ref.at[pl.ds(base + r, 1)], sem).start()
        return carry

    lax.fori_loop(0, n, issue, 0)
    pltpu.make_async_copy(src_ref.at[pl.ds(0, n)], out_ref.at[pl.ds(base, n)], sem).wait()


def _row_gather(src, idx):
    n_out = idx.shape[0]
    n = GATHER_ROWS
    idx3 = idx.reshape(n_out // n, 1, n)
    return pl.pallas_call(
        _gather_kernel,
        grid=(n_out // n,),
        in_specs=[pl.BlockSpec((1, 1, n), lambda i: (i, 0, 0), memory_space=pltpu.SMEM),
                  pl.BlockSpec(memory_space=pl.ANY)],
        out_specs=pl.BlockSpec(memory_space=pl.ANY),
        out_shape=jax.ShapeDtypeStruct((n_out,) + src.shape[1:], src.dtype),
        scratch_shapes=[pltpu.SemaphoreType.DMA],
        compiler_params=_params("arbitrary"),
        name="row_gather",
    )(idx3, src)


def _expert_kernel(te_ref, nt_ref, x_ref, wg_ref, wu_ref, wd_ref, o_ref, acc_ref):
    i = pl.program_id(0)
    j = pl.program_id(1)
    last = pl.num_programs(1) - 1
    used = i < nt_ref[0]

    @pl.when(used)
    def _():
        x = x_ref[...].astype(BF16)
        g = jnp.dot(x, wg_ref[0], preferred_element_type=F32)
        u = jnp.dot(x, wu_ref[0], preferred_element_type=F32)
        act = (g * jax.nn.sigmoid(g) * u).astype(BF16)
        part = jnp.dot(act, wd_ref[0], preferred_element_type=F32)

        @pl.when(j == 0)
        def _():
            acc_ref[...] = part

        @pl.when(j > 0)
        def _():
            acc_ref[...] += part

        @pl.when(j == last)
        def _():
            o_ref[...] = acc_ref[...]

    @pl.when(jnp.logical_not(used) & (j == last))
    def _():
        o_ref[...] = jnp.zeros_like(o_ref)


def _experts(xs, tile_expert, n_tiles_used, w_gate, w_up, w_down):
    P, D = xs.shape
    E, _, F = w_gate.shape
    tm, tf = EXPERT_TILE, EXPERT_F_TILE
    n_f = F // tf

    def row_map(i, j, te, nt):
        return (jnp.minimum(i, nt[0] - 1), 0)

    def col_w(i, j, te, nt):
        used = i < nt[0]
        return (te[i], 0, jnp.where(used, j, n_f - 1))

    def row_w(i, j, te, nt):
        used = i < nt[0]
        return (te[i], jnp.where(used, j, n_f - 1), 0)

    grid_spec = pltpu.PrefetchScalarGridSpec(
        num_scalar_prefetch=2,
        grid=(P // tm, n_f),
        in_specs=[pl.BlockSpec((tm, D), row_map),
                  pl.BlockSpec((1, D, tf), col_w),
                  pl.BlockSpec((1, D, tf), col_w),
                  pl.BlockSpec((1, tf, D), row_w)],
        out_specs=pl.BlockSpec((tm, D), lambda i, j, te, nt: (i, 0)),
        scratch_shapes=[pltpu.VMEM((tm, D), F32)],
    )
    return pl.pallas_call(
        _expert_kernel,
        grid_spec=grid_spec,
        out_shape=jax.ShapeDtypeStruct((P, D), F32),
        compiler_params=_params("arbitrary", "arbitrary"),
        name="expert_swiglu",
    )(tile_expert, n_tiles_used, xs, w_gate, w_up, w_down)


def _combine_kernel(h_ref, y1_ref, y2_ref, info_ref, p_ref, gp_ref, wpg_ref, wpp_ref, o_ref):
    info = info_ref[...]
    h = h_ref[...] + info[:, 2:3] * y1_ref[...] + info[:, 3:4] * y2_ref[...]
    o_ref[...] = _ple(h, p_ref[...], gp_ref[...], wpg_ref[...], wpp_ref[...])


def _combine(h, yk, info, p2, g_ple, w_pg, w_pp):
    T, D = h.shape
    PD = p2.shape[1]
    tm = ROW_TILE
    nb = T // tm
    row = lambda w: pl.BlockSpec((tm, w), lambda i: (i, 0))
    return pl.pallas_call(
        _combine_kernel,
        grid=(nb,),
        in_specs=[row(D), row(D), pl.BlockSpec((tm, D), lambda i: (nb + i, 0)), row(LANES), row(PD),
                  _const_spec((1, D)), _const_spec((D, D)), _const_spec((PD, D))],
        out_specs=row(D),
        out_shape=jax.ShapeDtypeStruct((T, D), F32),
        compiler_params=_params("parallel"),
        name="combine_ple",
    )(h, yk, yk, info, p2, g_ple, w_pg, w_pp)


def _rope_tables(S):
    inv = 1.0 / (ROPE_THETA ** (jnp.arange(0, DH, 2, dtype=F32) / DH))
    ang = jnp.arange(S, dtype=F32)[:, None] * inv[None, :]
    cos, sin = jnp.cos(ang), jnp.sin(ang)
    reps = HEAD_W // DH
    cos_t = jnp.tile(jnp.concatenate([cos, cos], axis=1), (1, reps))
    sin_t = jnp.tile(jnp.concatenate([-sin, sin], axis=1), (1, reps))
    return cos_t, sin_t


def _routing_tables(info, counts_f, n_experts, T):
    tm = EXPERT_TILE
    e12 = info[:, 0:2].astype(jnp.int32)
    rank = info[:, 4:6].astype(jnp.int32)
    counts = counts_f[0, :n_experts].astype(jnp.int32)
    padded = (counts + tm - 1) // tm * tm
    ends = jnp.cumsum(padded)
    starts = ends - padded
    onehot = e12[:, :, None] == jnp.arange(n_experts, dtype=jnp.int32)
    pos = jnp.sum(jnp.where(onehot, starts, 0), axis=-1) + rank
    pos_flat = pos.T.reshape(-1)
    P = T * TOP_K + n_experts * tm
    tok = jnp.tile(jnp.arange(T, dtype=jnp.int32), TOP_K)
    src_row = jnp.zeros((P,), jnp.int32).at[pos_flat].set(tok)
    tile_start = jnp.arange(P // tm, dtype=jnp.int32) * tm
    tile_expert = jnp.minimum(jnp.sum(tile_start[:, None] >= ends[None, :], axis=1), n_experts - 1)
    n_tiles_used = (ends[-1] // tm).reshape(1)
    return src_row, pos_flat, tile_expert.astype(jnp.int32), n_tiles_used.astype(jnp.int32)


def kernel(x, p, norm_mix, norm_ffn, norm_ple, attn_w_qkv, attn_w_o, attn_q_norm, attn_k_norm,
           attn_lambda_q1, attn_lambda_k1, attn_lambda_q2, attn_lambda_k2, attn_subln,
           pool_w, pool_b, pool_scale, ffn_w_gate, ffn_w_up, ffn_w_down,
           moe_router, moe_w_gate, moe_w_up, moe_w_down, ple_w_proj, ple_w_gate):
    B, S, D = x.shape
    T = B * S
    n_experts = moe_router.shape[-1]
    x2 = x.reshape(T, D)
    p2 = p.reshape(p.shape[0], T, p.shape[-1])
    bf = lambda a: a.astype(BF16)
    vec = lambda a: a.reshape(1, -1)

    lam_init = 0.8 - 0.6 * math.exp(-0.3 * 0)
    cos_t, sin_t = _rope_tables(S)
    reps = HEAD_W // DH
    qk_gain = jnp.zeros((8, LANES), F32)
    qk_gain = qk_gain.at[0].set(jnp.tile(attn_q_norm[0], reps) * ATTN_SCALE)
    qk_gain = qk_gain.at[1].set(jnp.tile(attn_k_norm[0], reps))
    group = jnp.arange(2 * LANES) // DH
    ones = (group[:, None] == group[None, :]).astype(BF16)
    qkv = _qkv(x2, vec(norm_mix[0]), bf(attn_w_qkv[0]), cos_t, sin_t, qk_gain, ones)
    lam_params = jnp.stack([attn_lambda_q1[0], attn_lambda_k1[0], attn_lambda_q2[0], attn_lambda_k2[0]])
    ao = _attention(qkv.reshape(B, S, 3 * D), lam_params, vec(attn_subln[0]), lam_init)
    h = _dense_tail(ao.reshape(T, D), x2, p2[0], bf(attn_w_o[0]), vec(norm_ffn[0]),
                    bf(ffn_w_gate[0]), bf(ffn_w_up[0]), bf(ffn_w_down[0]),
                    vec(norm_ple[0]), bf(ple_w_gate[0]), bf(ple_w_proj[0]))

    w_router_pad = jnp.zeros((D, LANES), F32).at[:, :n_experts].set(moe_router[0])
    h, xn, info, counts = _pool_router(h, S, vec(norm_mix[1]), bf(pool_w[0]), vec(pool_b[0]),
                                       vec(pool_scale[0]), vec(norm_ffn[1]), w_router_pad, n_experts)
    src_row, pos_flat, tile_expert, n_tiles_used = _routing_tables(info, counts, n_experts, T)
    xs = _row_gather(xn, src_row)
    ys = _experts(xs, tile_expert, n_tiles_used, bf(moe_w_gate[0]), bf(moe_w_up[0]), bf(moe_w_down[0]))
    yk = _row_gather(ys, pos_flat)
    out = _combine(h, yk, info, p2[1], vec(norm_ple[1]), bf(ple_w_gate[1]), bf(ple_w_proj[1]))
    return out.reshape(B, S, D)
```

```python
import functools
import math

import jax
import jax.numpy as jnp
from jax import lax
from jax.experimental import pallas as pl
from jax.experimental.pallas import tpu as pltpu

F32 = jnp.float32
BF16 = jnp.bfloat16

DH = 64
HEAD_W = 2 * DH
ATTN_SCALE = 1.0 / math.sqrt(DH)
ROPE_THETA = 10000.0
POOL_WINDOWS = (2, 4, 8, 16)
POOL_HALO = 16
TOP_K = 2
RMS_EPS = 1e-6
LANES = 128
SUBLANES = 8
VMEM_LIMIT = 56 * 1024 * 1024

ROW_TILE = 512
Q_TILE = 256
EXPERT_TILE = 512
EXPERT_F_TILE = 1792
GATHER_ROWS = 1024


def _rms(x, gain):
    return x * lax.rsqrt(jnp.mean(x * x, axis=-1, keepdims=True) + RMS_EPS) * gain


def _const_spec(shape):
    zeros = (0,) * len(shape)
    return pl.BlockSpec(shape, lambda *_: zeros, pipeline_mode=pl.Buffered(1))


def _to_rows3(ref, val):
    for c in range(SUBLANES):
        ref[:, c, :] = val[:, c * LANES:(c + 1) * LANES]


def _from_rows3(ref):
    return jnp.concatenate([ref[:, c, :] for c in range(SUBLANES)], axis=1)


def _params(*sem):
    return pltpu.CompilerParams(dimension_semantics=sem, vmem_limit_bytes=VMEM_LIMIT)


def _qkv_kernel(x_ref, g_ref, w_ref, cos_ref, sin_ref, qkg_ref, ones_ref, o_ref):
    D = x_ref.shape[1]
    xn = _rms(x_ref[...], g_ref[...]).astype(BF16)
    cos = cos_ref[...]
    sin = sin_ref[...]
    lane = lax.broadcasted_iota(jnp.int32, cos.shape, 1)
    first_half = (lane % DH) < (DH // 2)
    ones = ones_ref[...]
    width = ones.shape[0]
    for part in range(3):
        y = jnp.dot(xn, w_ref[:, part * D:(part + 1) * D], preferred_element_type=F32)
        if part == 2:
            o_ref[:, part * D:(part + 1) * D] = y.astype(BF16)
            continue
        gain = qkg_ref[part:part + 1, :]
        for c in range(0, D, width):
            yc = y[:, c:c + width]
            yy = yc * yc
            hi = yy.astype(BF16)
            lo = (yy - hi.astype(F32)).astype(BF16)
            ss = (jnp.dot(hi, ones, preferred_element_type=F32)
                  + jnp.dot(lo, ones, preferred_element_type=F32))
            yn = yc * lax.rsqrt(ss * (1.0 / DH) + RMS_EPS)
            for s in range(0, width, LANES):
                z = yn[:, s:s + LANES] * gain
                rot = jnp.where(first_half,
                                pltpu.roll(z, LANES - DH // 2, 1),
                                pltpu.roll(z, DH // 2, 1))
                col = part * D + c + s
                o_ref[:, col:col + LANES] = (z * cos + rot * sin).astype(BF16)


def _qkv(x2, gain, w_qkv, cos_t, sin_t, qk_gain, ones):
    T, D = x2.shape
    S = cos_t.shape[0]
    tm = ROW_TILE
    blocks_per_seq = S // tm
    return pl.pallas_call(
        _qkv_kernel,
        grid=(T // tm,),
        in_specs=[
            pl.BlockSpec((tm, D), lambda i: (i, 0)),
            _const_spec((1, D)),
            _const_spec((D, 3 * D)),
            pl.BlockSpec((tm, LANES), lambda i: (i % blocks_per_seq, 0)),
            pl.BlockSpec((tm, LANES), lambda i: (i % blocks_per_seq, 0)),
            _const_spec(qk_gain.shape),
            _const_spec(ones.shape),
        ],
        out_specs=pl.BlockSpec((tm, 3 * D), lambda i: (i, 0)),
        out_shape=jax.ShapeDtypeStruct((T, 3 * D), BF16),
        compiler_params=_params("parallel"),
        name="qkv_rope",
    )(x2, gain, w_qkv, cos_t, sin_t, qk_gain, ones)


def _dot_nt(a, b):
    return lax.dot_general(a, b, (((1,), (1,)), ((), ())), preferred_element_type=F32)


def _attn_kernel(q_ref, k_ref, v_ref, lam_ref, g_ref, o_ref, vext_ref, *, lam_init):
    lp = lam_ref[...]
    lam = (jnp.exp(jnp.sum(lp[0:1] * lp[1:2], axis=-1, keepdims=True))
           - jnp.exp(jnp.sum(lp[2:3] * lp[3:4], axis=-1, keepdims=True)) + lam_init)
    S = q_ref.shape[1]
    tq = Q_TILE
    vlane = lax.broadcasted_iota(jnp.int32, (S, HEAD_W), 1)
    vext_ref[:, :HEAD_W] = v_ref[0]
    vext_ref[:, HEAD_W:] = jnp.where(vlane == 0, 1.0, 0.0).astype(BF16)
    lane = lax.broadcasted_iota(jnp.int32, (tq, HEAD_W), 1)
    row = lax.broadcasted_iota(jnp.int32, (tq, tq), 0)
    col = lax.broadcasted_iota(jnp.int32, (tq, tq), 1)
    future = col > row
    gain = g_ref[...]
    for i in range(S // tq):
        q = q_ref[0, i * tq:(i + 1) * tq, :]
        kv = (i + 1) * tq
        qs = (jnp.where(lane < DH, q, jnp.zeros_like(q)), jnp.where(lane >= DH, q, jnp.zeros_like(q)))
        acc = []
        for qc in qs:
            s_diag = jnp.where(future, -jnp.inf, _dot_nt(qc, k_ref[0, i * tq:kv, :]))
            m = jnp.max(s_diag, axis=-1, keepdims=True)
            if i > 0:
                s_off = _dot_nt(qc, k_ref[0, :i * tq, :])
                m = jnp.maximum(m, jnp.max(s_off, axis=-1, keepdims=True))
                p = jnp.concatenate([jnp.exp2(s_off - m).astype(BF16),
                                     jnp.exp2(s_diag - m).astype(BF16)], axis=1)
            else:
                p = jnp.exp2(s_diag - m).astype(BF16)
            acc.append(jnp.dot(p, vext_ref[:kv, :], preferred_element_type=F32))
        o = (acc[0][:, :HEAD_W] * (1.0 / acc[0][:, HEAD_W:HEAD_W + 1])
             - acc[1][:, :HEAD_W] * (lam / acc[1][:, HEAD_W:HEAD_W + 1]))
        o = _rms(o, gain) * (1.0 - lam_init)
        o_ref[0, i * tq:(i + 1) * tq, :] = o.astype(BF16)


def _attention(qkv3, lam_params, subln, lam_init):
    B, S, W3 = qkv3.shape
    H = W3 // 3 // HEAD_W
    blk = (1, S, HEAD_W)
    return pl.pallas_call(
        functools.partial(_attn_kernel, lam_init=lam_init),
        grid=(B, H),
        in_specs=[
            pl.BlockSpec(blk, lambda b, h: (b, 0, h)),
            pl.BlockSpec(blk, lambda b, h: (b, 0, H + h)),
            pl.BlockSpec(blk, lambda b, h: (b, 0, 2 * H + h)),
            _const_spec(lam_params.shape),
            _const_spec(subln.shape),
        ],
        out_specs=pl.BlockSpec(blk, lambda b, h: (b, 0, h)),
        out_shape=jax.ShapeDtypeStruct((B, S, H * HEAD_W), BF16),
        scratch_shapes=[pltpu.VMEM((S, 2 * HEAD_W), BF16)],
        compiler_params=_params("parallel", "parallel"),
        name="diff_attn",
    )(qkv3, qkv3, qkv3, lam_params, subln)


def _ple(h, p, g_ple, w_gate, w_proj):
    xn = _rms(h, g_ple).astype(BF16)
    gate = jax.nn.sigmoid(jnp.dot(xn, w_gate, preferred_element_type=F32))
    proj = jnp.dot(p.astype(BF16), w_proj, preferred_element_type=F32)
    return h + gate * proj


def _dense_tail_kernel(ao_ref, x_ref, p_ref, wo_ref, gf_ref, wg_ref, wu_ref, wd_ref,
                       gp_ref, wpg_ref, wpp_ref, o_ref, *, f_chunks):
    h = x_ref[...] + jnp.dot(ao_ref[...], wo_ref[...], preferred_element_type=F32)
    xn = _rms(h, gf_ref[...]).astype(BF16)
    acc = h
    for lo, hi in f_chunks:
        g = jnp.dot(xn, wg_ref[:, lo:hi], preferred_element_type=F32)
        u = jnp.dot(xn, wu_ref[:, lo:hi], preferred_element_type=F32)
        act = (g * jax.nn.sigmoid(g) * u).astype(BF16)
        acc = acc + jnp.dot(act, wd_ref[lo:hi, :], preferred_element_type=F32)
    o_ref[...] = _ple(acc, p_ref[...], gp_ref[...], wpg_ref[...], wpp_ref[...])


def _dense_tail(ao, x2, p3, layer, w_o, g_ffn, w_gate, w_up, w_down, g_ple, w_pg, w_pp):
    T, D = x2.shape
    F = w_gate.shape[1]
    PD = p3.shape[2]
    tm = ROW_TILE
    f_chunks = tuple((lo, min(lo + 1024, F)) for lo in range(0, F, 1024))
    row = lambda w: pl.BlockSpec((tm, w), lambda i: (i, 0))
    return pl.pallas_call(
        functools.partial(_dense_tail_kernel, f_chunks=f_chunks),
        grid=(T // tm,),
        in_specs=[row(D), row(D), pl.BlockSpec((None, tm, PD), lambda i: (layer, i, 0)),
                  _const_spec((D, D)), _const_spec((1, D)),
                  _const_spec((D, F)), _const_spec((D, F)), _const_spec((F, D)),
                  _const_spec((1, D)), _const_spec((D, D)), _const_spec((PD, D))],
        out_specs=row(D),
        out_shape=jax.ShapeDtypeStruct((T, D), F32),
        compiler_params=_params("parallel"),
        name="dense_tail",
    )(ao, x2, p3, w_o, g_ffn, w_gate, w_up, w_down, g_ple, w_pg, w_pp)


def _pool_router_kernel(h_ref, halo_ref, gm_ref, pw_ref, pb_ref, ps_ref, gf_ref, wr_ref,
                        h_out_ref, xn_out_ref, info_ref, cnt_ref, xbuf_ref, carry_ref,
                        *, blocks_per_seq, n_experts):
    i = pl.program_id(0)
    tm, D = h_ref.shape
    gw = D // len(POOL_WINDOWS)
    seq_block = i % blocks_per_seq

    @pl.when(i == 0)
    def _():
        carry_ref[...] = jnp.zeros_like(carry_ref)

    h = h_ref[...]
    gm = gm_ref[...]
    xbuf_ref[POOL_HALO:, :] = _rms(h, gm)

    @pl.when(seq_block == 0)
    def _():
        xbuf_ref[:POOL_HALO, :] = jnp.zeros((POOL_HALO, D), F32)

    @pl.when(seq_block != 0)
    def _():
        xbuf_ref[:POOL_HALO, :] = _rms(halo_ref[...], gm)

    t = seq_block * tm + lax.broadcasted_iota(jnp.int32, (tm, 1), 0)
    pieces = []
    for g, w in enumerate(POOL_WINDOWS):
        cs = slice(g * gw, (g + 1) * gw)
        win = xbuf_ref[POOL_HALO:, cs]
        tok = win
        for k in range(1, w):
            win = win + xbuf_ref[POOL_HALO - k:POOL_HALO - k + tm, cs]
        cnt = jnp.minimum(t + 1, w).astype(F32)
        y = (win / cnt - tok).astype(BF16)
        pieces.append(jnp.dot(y, pw_ref[g], preferred_element_type=F32))
    mix = (jnp.concatenate(pieces, axis=1) + pb_ref[...]) * ps_ref[...]
    h2 = h + mix
    h_out_ref[...] = h2
    xn = _rms(h2, gf_ref[...])
    _to_rows3(xn_out_ref, xn)

    logits = jnp.dot(xn, wr_ref[...], preferred_element_type=F32, precision=lax.Precision.HIGHEST)
    lane = lax.broadcasted_iota(jnp.int32, logits.shape, 1)
    lg = jnp.where(lane < n_experts, logits, -jnp.inf)
    m1 = jnp.max(lg, axis=-1, keepdims=True)
    i1 = jnp.min(jnp.where(lg == m1, lane, LANES), axis=-1, keepdims=True)
    lg2 = jnp.where(lane == i1, -jnp.inf, lg)
    m2 = jnp.max(lg2, axis=-1, keepdims=True)
    i2 = jnp.min(jnp.where(lg2 == m2, lane, LANES), axis=-1, keepdims=True)
    e = jnp.exp(m2 - m1)
    w1 = 1.0 / (1.0 + e)
    w2 = e / (1.0 + e)

    sel1 = lane == i1
    sel2 = lane == i2
    onehot = jnp.where(sel1 | sel2, 1.0, 0.0)
    r = lax.broadcasted_iota(jnp.int32, (tm, tm), 0)
    c = lax.broadcasted_iota(jnp.int32, (tm, tm), 1)
    strict_lower = jnp.where(c < r, 1.0, 0.0).astype(BF16)
    before = jnp.dot(strict_lower, onehot.astype(BF16), preferred_element_type=F32) + carry_ref[0:1, :]
    rank1 = jnp.sum(jnp.where(sel1, before, 0.0), axis=-1, keepdims=True)
    rank2 = jnp.sum(jnp.where(sel2, before, 0.0), axis=-1, keepdims=True)
    total = carry_ref[0:1, :] + jnp.sum(onehot, axis=0, keepdims=True)
    carry_ref[...] = jnp.broadcast_to(total, carry_ref.shape)
    cnt_ref[...] = jnp.broadcast_to(total, cnt_ref.shape)

    info = jnp.where(lane == 0, i1.astype(F32), 0.0)
    info = jnp.where(lane == 1, i2.astype(F32), info)
    info = jnp.where(lane == 2, w1, info)
    info = jnp.where(lane == 3, w2, info)
    info = jnp.where(lane == 4, rank1, info)
    info = jnp.where(lane == 5, rank2, info)
    info_ref[...] = info


def _pool_router(h, S, g_mix, pool_w, pool_b, pool_scale, g_ffn, w_router_pad, n_experts):
    T, D = h.shape
    tm = ROW_TILE
    G, gw, _ = pool_w.shape
    halo_per_tile = tm // POOL_HALO
    row = pl.BlockSpec((tm, D), lambda i: (i, 0))
    return pl.pallas_call(
        functools.partial(_pool_router_kernel, blocks_per_seq=S // tm, n_experts=n_experts),
        grid=(T // tm,),
        in_specs=[
            row,
            pl.BlockSpec((POOL_HALO, D), lambda i: (jnp.maximum(i * halo_per_tile - 1, 0), 0)),
            _const_spec((1, D)), _const_spec((G, gw, gw)), _const_spec((1, D)), _const_spec((1, D)),
            _const_spec((1, D)), _const_spec((D, LANES)),
        ],
        out_specs=[row, pl.BlockSpec((tm, SUBLANES, LANES), lambda i: (i, 0, 0)),
                   pl.BlockSpec((tm, LANES), lambda i: (i, 0)),
                   pl.BlockSpec((8, LANES), lambda i: (0, 0))],
        out_shape=[jax.ShapeDtypeStruct((T, D), F32), jax.ShapeDtypeStruct((T, SUBLANES, LANES), F32),
                   jax.ShapeDtypeStruct((T, LANES), F32), jax.ShapeDtypeStruct((8, LANES), F32)],
        scratch_shapes=[pltpu.VMEM((tm + POOL_HALO, D), F32), pltpu.VMEM((8, LANES), F32)],
        compiler_params=_params("arbitrary"),
        name="pool_router",
    )(h, h, g_mix, pool_w, pool_b, pool_scale, g_ffn, w_router_pad)


def _gather_kernel(idx_ref, src_ref, out_ref, sem):
    n = out_ref.shape[0]

    def issue(r, carry):
        pltpu.make_async_copy(src_ref.at[pl.ds(idx_ref[0, 0, r], 1)], out_ref.at[pl.ds(r, 1)], sem).start()
        return carry

    lax.fori_loop(0, n, issue, 0, unroll=8)
    pltpu.make_async_copy(src_ref.at[pl.ds(0, n)], out_ref, sem).wait()


def _row_gather(src, idx):
    n_out = idx.shape[0]
    n = GATHER_ROWS
    idx3 = idx.reshape(n_out // n, 1, n)
    tail = (0,) * (src.ndim - 1)
    return pl.pallas_call(
        _gather_kernel,
        grid=(n_out // n,),
        in_specs=[pl.BlockSpec((1, 1, n), lambda i: (i, 0, 0), memory_space=pltpu.SMEM),
                  pl.BlockSpec(memory_space=pl.ANY)],
        out_specs=pl.BlockSpec((n,) + src.shape[1:], lambda i: (i,) + tail),
        out_shape=jax.ShapeDtypeStruct((n_out,) + src.shape[1:], src.dtype),
        scratch_shapes=[pltpu.SemaphoreType.DMA],
        compiler_params=_params("arbitrary"),
        name="row_gather",
    )(idx3, src)


def _expert_kernel(te_ref, nt_ref, x_ref, wg_ref, wu_ref, wd_ref, o_ref, acc_ref, xb_ref):
    i = pl.program_id(0)
    j = pl.program_id(1)
    last = pl.num_programs(1) - 1
    used = i < nt_ref[0]

    @pl.when(used & (j == 0))
    def _():
        xb_ref[...] = _from_rows3(x_ref).astype(BF16)

    @pl.when(used)
    def _():
        x = xb_ref[...]
        g = jnp.dot(x, wg_ref[0], preferred_element_type=F32)
        u = jnp.dot(x, wu_ref[0], preferred_element_type=F32)
        act = (g * jax.nn.sigmoid(g) * u).astype(BF16)
        part = jnp.dot(act, wd_ref[0], preferred_element_type=F32)

        @pl.when(j == 0)
        def _():
            acc_ref[...] = part

        @pl.when(j > 0)
        def _():
            acc_ref[...] += part

        @pl.when(j == last)
        def _():
            o_ref[...] = acc_ref[...]

    @pl.when(jnp.logical_not(used) & (j == last))
    def _():
        o_ref[...] = jnp.zeros_like(o_ref)


def _experts(xs, tile_expert, n_tiles_used, w_gate, w_up, w_down):
    P = xs.shape[0]
    E, D, F = w_gate.shape
    tm, tf = EXPERT_TILE, EXPERT_F_TILE
    n_f = F // tf

    def row_map(i, j, te, nt):
        return (jnp.minimum(i, nt[0] - 1), 0, 0)

    def col_w(i, j, te, nt):
        used = i < nt[0]
        return (te[i], 0, jnp.where(used, j, n_f - 1))

    def row_w(i, j, te, nt):
        used = i < nt[0]
        return (te[i], jnp.where(used, j, n_f - 1), 0)

    grid_spec = pltpu.PrefetchScalarGridSpec(
        num_scalar_prefetch=2,
        grid=(P // tm, n_f),
        in_specs=[pl.BlockSpec((tm, SUBLANES, LANES), row_map),
                  pl.BlockSpec((1, D, tf), col_w),
                  pl.BlockSpec((1, D, tf), col_w),
                  pl.BlockSpec((1, tf, D), row_w)],
        out_specs=pl.BlockSpec((tm, D), lambda i, j, te, nt: (i, 0)),
        scratch_shapes=[pltpu.VMEM((tm, D), F32), pltpu.VMEM((tm, D), BF16)],
    )
    return pl.pallas_call(
        _expert_kernel,
        grid_spec=grid_spec,
        out_shape=jax.ShapeDtypeStruct((P, D), F32),
        compiler_params=_params("arbitrary", "arbitrary"),
        name="expert_swiglu",
    )(tile_expert, n_tiles_used, xs, w_gate, w_up, w_down)


def _combine_kernel(h_ref, y1_ref, y2_ref, info_ref, p_ref, gp_ref, wpg_ref, wpp_ref, o_ref):
    info = info_ref[...]
    h = h_ref[...] + info[:, 2:3] * y1_ref[...] + info[:, 3:4] * y2_ref[...]
    o_ref[...] = _ple(h, p_ref[...], gp_ref[...], wpg_ref[...], wpp_ref[...])


def _combine(h, yk, info, p3, layer, g_ple, w_pg, w_pp):
    T, D = h.shape
    PD = p3.shape[2]
    tm = ROW_TILE
    nb = T // tm
    row = lambda w: pl.BlockSpec((tm, w), lambda i: (i, 0))
    return pl.pallas_call(
        _combine_kernel,
        grid=(nb,),
        in_specs=[row(D), row(D), pl.BlockSpec((tm, D), lambda i: (nb + i, 0)), row(LANES),
                  pl.BlockSpec((None, tm, PD), lambda i: (layer, i, 0)),
                  _const_spec((1, D)), _const_spec((D, D)), _const_spec((PD, D))],
        out_specs=row(D),
        out_shape=jax.ShapeDtypeStruct((T, D), F32),
        compiler_params=_params("parallel"),
        name="combine_ple",
    )(h, yk, yk, info, p3, g_ple, w_pg, w_pp)


def _rope_tables(S):
    inv = 1.0 / (ROPE_THETA ** (jnp.arange(0, DH, 2, dtype=F32) / DH))
    ang = jnp.arange(S, dtype=F32)[:, None] * inv[None, :]
    cos, sin = jnp.cos(ang), jnp.sin(ang)
    reps = HEAD_W // DH
    cos_t = jnp.tile(jnp.concatenate([cos, cos], axis=1), (1, reps))
    sin_t = jnp.tile(jnp.concatenate([-sin, sin], axis=1), (1, reps))
    return cos_t, sin_t


def _routing_tables(info, counts_f, n_experts, T):
    tm = EXPERT_TILE
    e12 = info[:, 0:2].astype(jnp.int32)
    rank = info[:, 4:6].astype(jnp.int32)
    counts = counts_f[0, :n_experts].astype(jnp.int32)
    padded = (counts + tm - 1) // tm * tm
    ends = jnp.cumsum(padded)
    starts = ends - padded
    onehot = e12[:, :, None] == jnp.arange(n_experts, dtype=jnp.int32)
    pos = jnp.sum(jnp.where(onehot, starts, 0), axis=-1) + rank
    pos_flat = pos.T.reshape(-1)
    P = T * TOP_K + n_experts * tm
    tok = jnp.tile(jnp.arange(T, dtype=jnp.int32), TOP_K)
    src_row = jnp.zeros((P,), jnp.int32).at[pos_flat].set(tok)
    tile_start = jnp.arange(P // tm, dtype=jnp.int32) * tm
    tile_expert = jnp.minimum(jnp.sum(tile_start[:, None] >= ends[None, :], axis=1), n_experts - 1)
    n_tiles_used = (ends[-1] // tm).reshape(1)
    return src_row, pos_flat, tile_expert.astype(jnp.int32), n_tiles_used.astype(jnp.int32)


def kernel(x, p, norm_mix, norm_ffn, norm_ple, attn_w_qkv, attn_w_o, attn_q_norm, attn_k_norm,
           attn_lambda_q1, attn_lambda_k1, attn_lambda_q2, attn_lambda_k2, attn_subln,
           pool_w, pool_b, pool_scale, ffn_w_gate, ffn_w_up, ffn_w_down,
           moe_router, moe_w_gate, moe_w_up, moe_w_down, ple_w_proj, ple_w_gate):
    B, S, D = x.shape
    T = B * S
    n_experts = moe_router.shape[-1]
    x2 = x.reshape(T, D)
    p3 = p.reshape(p.shape[0], T, p.shape[-1])
    bf = lambda a: a.astype(BF16)
    vec = lambda a: a.reshape(1, -1)

    lam_init = 0.8 - 0.6 * math.exp(-0.3 * 0)
    cos_t, sin_t = _rope_tables(S)
    reps = HEAD_W // DH
    qk_gain = jnp.zeros((8, LANES), F32)
    qk_gain = qk_gain.at[0].set(jnp.tile(attn_q_norm[0], reps) * (ATTN_SCALE * math.log2(math.e)))
    qk_gain = qk_gain.at[1].set(jnp.tile(attn_k_norm[0], reps))
    group = jnp.arange(2 * LANES) // DH
    ones = (group[:, None] == group[None, :]).astype(BF16)
    qkv = _qkv(x2, vec(norm_mix[0]), bf(attn_w_qkv[0]), cos_t, sin_t, qk_gain, ones)
    lam_params = jnp.stack([attn_lambda_q1[0], attn_lambda_k1[0], attn_lambda_q2[0], attn_lambda_k2[0]])
    ao = _attention(qkv.reshape(B, S, 3 * D), lam_params, vec(attn_subln[0]), lam_init)
    h = _dense_tail(ao.reshape(T, D), x2, p3, 0, bf(attn_w_o[0]), vec(norm_ffn[0]),
                    bf(ffn_w_gate[0]), bf(ffn_w_up[0]), bf(ffn_w_down[0]),
                    vec(norm_ple[0]), bf(ple_w_gate[0]), bf(ple_w_proj[0]))

    w_router_pad = jnp.zeros((D, LANES), F32).at[:, :n_experts].set(moe_router[0])
    h, xn, info, counts = _pool_router(h, S, vec(norm_mix[1]), bf(pool_w[0]), vec(pool_b[0]),
                                       vec(pool_scale[0]), vec(norm_ffn[1]), w_router_pad, n_experts)
    src_row, pos_flat, tile_expert, n_tiles_used = _routing_tables(info, counts, n_experts, T)
    xs = _row_gather(xn, src_row)
    ys = _experts(xs, tile_expert, n_tiles_used, bf(moe_w_gate[0]), bf(moe_w_up[0]), bf(moe_w_down[0]))
    yk = _row_gather(ys, pos_flat)
    out = _combine(h, yk, info, p3, 1, vec(norm_ple[1]), bf(ple_w_gate[1]), bf(ple_w_proj[1]))
    return out.reshape(B, S, D)
```

```python
import functools
import math

import jax
import jax.numpy as jnp
from jax import lax
from jax.experimental import pallas as pl
from jax.experimental.pallas import tpu as pltpu

F32 = jnp.float32
BF16 = jnp.bfloat16

DH = 64
HEAD_W = 2 * DH
ATTN_SCALE = 1.0 / math.sqrt(DH)
ROPE_THETA = 10000.0
POOL_WINDOWS = (2, 4, 8, 16)
POOL_HALO = 16
TOP_K = 2
RMS_EPS = 1e-6
LANES = 128
VMEM_LIMIT = 56 * 1024 * 1024

ROW_TILE = 512
Q_TILE = 256
EXPERT_TILE = 512
EXPERT_F_TILE = 1792
GATHER_ROWS = 1024


def _rms(x, gain):
    return x * lax.rsqrt(jnp.mean(x * x, axis=-1, keepdims=True) + RMS_EPS) * gain


def _const_spec(shape):
    zeros = (0,) * len(shape)
    return pl.BlockSpec(shape, lambda *_: zeros, pipeline_mode=pl.Buffered(1))


def _params(*sem):
    return pltpu.CompilerParams(dimension_semantics=sem, vmem_limit_bytes=VMEM_LIMIT)


def _qkv_kernel(x_ref, g_ref, w_ref, cos_ref, sin_ref, qkg_ref, ones_ref, o_ref):
    D = x_ref.shape[1]
    xn = _rms(x_ref[...], g_ref[...]).astype(BF16)
    cos = cos_ref[...]
    sin = sin_ref[...]
    lane = lax.broadcasted_iota(jnp.int32, cos.shape, 1)
    first_half = (lane % DH) < (DH // 2)
    ones = ones_ref[...]
    width = ones.shape[0]
    for part in range(3):
        y = jnp.dot(xn, w_ref[:, part * D:(part + 1) * D], preferred_element_type=F32)
        if part == 2:
            o_ref[:, part * D:(part + 1) * D] = y.astype(BF16)
            continue
        gain = qkg_ref[part:part + 1, :]
        for c in range(0, D, width):
            yc = y[:, c:c + width]
            yy = yc * yc
            hi = yy.astype(BF16)
            lo = (yy - hi.astype(F32)).astype(BF16)
            ss = (jnp.dot(hi, ones, preferred_element_type=F32)
                  + jnp.dot(lo, ones, preferred_element_type=F32))
            yn = yc * lax.rsqrt(ss * (1.0 / DH) + RMS_EPS)
            for s in range(0, width, LANES):
                z = yn[:, s:s + LANES] * gain
                rot = jnp.where(first_half,
                                pltpu.roll(z, LANES - DH // 2, 1),
                                pltpu.roll(z, DH // 2, 1))
                col = part * D + c + s
                o_ref[:, col:col + LANES] = (z * cos + rot * sin).astype(BF16)


def _qkv(x2, gain, w_qkv, cos_t, sin_t, qk_gain, ones):
    T, D = x2.shape
    S = cos_t.shape[0]
    tm = ROW_TILE
    blocks_per_seq = S // tm
    return pl.pallas_call(
        _qkv_kernel,
        grid=(T // tm,),
        in_specs=[
            pl.BlockSpec((tm, D), lambda i: (i, 0)),
            _const_spec((1, D)),
            _const_spec((D, 3 * D)),
            pl.BlockSpec((tm, LANES), lambda i: (i % blocks_per_seq, 0)),
            pl.BlockSpec((tm, LANES), lambda i: (i % blocks_per_seq, 0)),
            _const_spec(qk_gain.shape),
            _const_spec(ones.shape),
        ],
        out_specs=pl.BlockSpec((tm, 3 * D), lambda i: (i, 0)),
        out_shape=jax.ShapeDtypeStruct((T, 3 * D), BF16),
        compiler_params=_params("parallel"),
        name="qkv_rope",
    )(x2, gain, w_qkv, cos_t, sin_t, qk_gain, ones)


def _dot_nt(a, b):
    return lax.dot_general(a, b, (((1,), (1,)), ((), ())), preferred_element_type=F32)


def _attn_kernel(q_ref, k_ref, v_ref, lam_ref, g_ref, o_ref, vext_ref, *, lam_init):
    lp = lam_ref[...]
    lam = (jnp.exp(jnp.sum(lp[0:1] * lp[1:2], axis=-1, keepdims=True))
           - jnp.exp(jnp.sum(lp[2:3] * lp[3:4], axis=-1, keepdims=True)) + lam_init)
    S = q_ref.shape[1]
    tq = Q_TILE
    vlane = lax.broadcasted_iota(jnp.int32, (S, HEAD_W), 1)
    vext_ref[:, :HEAD_W] = v_ref[0]
    vext_ref[:, HEAD_W:] = jnp.where(vlane == 0, 1.0, 0.0).astype(BF16)
    lane = lax.broadcasted_iota(jnp.int32, (tq, HEAD_W), 1)
    row = lax.broadcasted_iota(jnp.int32, (tq, tq), 0)
    col = lax.broadcasted_iota(jnp.int32, (tq, tq), 1)
    future = col > row
    gain = g_ref[...]
    for i in range(S // tq):
        q = q_ref[0, i * tq:(i + 1) * tq, :]
        kv = (i + 1) * tq
        qs = (jnp.where(lane < DH, q, jnp.zeros_like(q)), jnp.where(lane >= DH, q, jnp.zeros_like(q)))
        acc = []
        for qc in qs:
            s_diag = jnp.where(future, -jnp.inf, _dot_nt(qc, k_ref[0, i * tq:kv, :]))
            m = jnp.max(s_diag, axis=-1, keepdims=True)
            if i > 0:
                s_off = _dot_nt(qc, k_ref[0, :i * tq, :])
                m = jnp.maximum(m, jnp.max(s_off, axis=-1, keepdims=True))
                p = jnp.concatenate([jnp.exp2(s_off - m).astype(BF16),
                                     jnp.exp2(s_diag - m).astype(BF16)], axis=1)
            else:
                p = jnp.exp2(s_diag - m).astype(BF16)
            acc.append(jnp.dot(p, vext_ref[:kv, :], preferred_element_type=F32))
        o = (acc[0][:, :HEAD_W] * (1.0 / acc[0][:, HEAD_W:HEAD_W + 1])
             - acc[1][:, :HEAD_W] * (lam / acc[1][:, HEAD_W:HEAD_W + 1]))
        o = _rms(o, gain) * (1.0 - lam_init)
        o_ref[0, i * tq:(i + 1) * tq, :] = o.astype(BF16)


def _attention(qkv3, lam_params, subln, lam_init):
    B, S, W3 = qkv3.shape
    H = W3 // 3 // HEAD_W
    blk = (1, S, HEAD_W)
    return pl.pallas_call(
        functools.partial(_attn_kernel, lam_init=lam_init),
        grid=(B, H),
        in_specs=[
            pl.BlockSpec(blk, lambda b, h: (b, 0, h)),
            pl.BlockSpec(blk, lambda b, h: (b, 0, H + h)),
            pl.BlockSpec(blk, lambda b, h: (b, 0, 2 * H + h)),
            _const_spec(lam_params.shape),
            _const_spec(subln.shape),
        ],
        out_specs=pl.BlockSpec(blk, lambda b, h: (b, 0, h)),
        out_shape=jax.ShapeDtypeStruct((B, S, H * HEAD_W), BF16),
        scratch_shapes=[pltpu.VMEM((S, 2 * HEAD_W), BF16)],
        compiler_params=_params("parallel", "parallel"),
        name="diff_attn",
    )(qkv3, qkv3, qkv3, lam_params, subln)


def _ple(h, p, g_ple, w_gate, w_proj):
    xn = _rms(h, g_ple).astype(BF16)
    gate = jax.nn.sigmoid(jnp.dot(xn, w_gate, preferred_element_type=F32))
    proj = jnp.dot(p.astype(BF16), w_proj, preferred_element_type=F32)
    return h + gate * proj


def _dense_tail_kernel(ao_ref, x_ref, p_ref, wo_ref, gf_ref, wg_ref, wu_ref, wd_ref,
                       gp_ref, wpg_ref, wpp_ref, o_ref, *, f_chunks):
    h = x_ref[...] + jnp.dot(ao_ref[...], wo_ref[...], preferred_element_type=F32)
    xn = _rms(h, gf_ref[...]).astype(BF16)
    acc = h
    for lo, hi in f_chunks:
        g = jnp.dot(xn, wg_ref[:, lo:hi], preferred_element_type=F32)
        u = jnp.dot(xn, wu_ref[:, lo:hi], preferred_element_type=F32)
        act = (g * jax.nn.sigmoid(g) * u).astype(BF16)
        acc = acc + jnp.dot(act, wd_ref[lo:hi, :], preferred_element_type=F32)
    o_ref[...] = _ple(acc, p_ref[...], gp_ref[...], wpg_ref[...], wpp_ref[...])


def _dense_tail(ao, x2, p3, layer, w_o, g_ffn, w_gate, w_up, w_down, g_ple, w_pg, w_pp):
    T, D = x2.shape
    F = w_gate.shape[1]
    PD = p3.shape[2]
    tm = ROW_TILE
    f_chunks = tuple((lo, min(lo + 1024, F)) for lo in range(0, F, 1024))
    row = lambda w: pl.BlockSpec((tm, w), lambda i: (i, 0))
    return pl.pallas_call(
        functools.partial(_dense_tail_kernel, f_chunks=f_chunks),
        grid=(T // tm,),
        in_specs=[row(D), row(D), pl.BlockSpec((None, tm, PD), lambda i: (layer, i, 0)),
                  _const_spec((D, D)), _const_spec((1, D)),
                  _const_spec((D, F)), _const_spec((D, F)), _const_spec((F, D)),
                  _const_spec((1, D)), _const_spec((D, D)), _const_spec((PD, D))],
        out_specs=row(D),
        out_shape=jax.ShapeDtypeStruct((T, D), F32),
        compiler_params=_params("parallel"),
        name="dense_tail",
    )(ao, x2, p3, w_o, g_ffn, w_gate, w_up, w_down, g_ple, w_pg, w_pp)


def _pool_router_kernel(h_ref, halo_ref, gm_ref, pw_ref, pb_ref, ps_ref, gf_ref, wr_ref,
                        h_out_ref, xn_out_ref, info_ref, cnt_ref, xbuf_ref, carry_ref,
                        *, blocks_per_seq, n_experts):
    i = pl.program_id(0)
    tm, D = h_ref.shape
    gw = D // len(POOL_WINDOWS)
    seq_block = i % blocks_per_seq

    @pl.when(i == 0)
    def _():
        carry_ref[...] = jnp.zeros_like(carry_ref)

    h = h_ref[...]
    gm = gm_ref[...]
    xbuf_ref[POOL_HALO:, :] = _rms(h, gm)

    @pl.when(seq_block == 0)
    def _():
        xbuf_ref[:POOL_HALO, :] = jnp.zeros((POOL_HALO, D), F32)

    @pl.when(seq_block != 0)
    def _():
        xbuf_ref[:POOL_HALO, :] = _rms(halo_ref[...], gm)

    t = seq_block * tm + lax.broadcasted_iota(jnp.int32, (tm, 1), 0)
    pieces = []
    for g, w in enumerate(POOL_WINDOWS):
        cs = slice(g * gw, (g + 1) * gw)
        win = xbuf_ref[POOL_HALO:, cs]
        tok = win
        for k in range(1, w):
            win = win + xbuf_ref[POOL_HALO - k:POOL_HALO - k + tm, cs]
        cnt = jnp.minimum(t + 1, w).astype(F32)
        y = (win / cnt - tok).astype(BF16)
        pieces.append(jnp.dot(y, pw_ref[g], preferred_element_type=F32))
    mix = (jnp.concatenate(pieces, axis=1) + pb_ref[...]) * ps_ref[...]
    h2 = h + mix
    h_out_ref[...] = h2
    xn = _rms(h2, gf_ref[...])
    xn_out_ref[...] = xn

    xh = xn.astype(BF16)
    xl = (xn - xh.astype(F32)).astype(BF16)
    logits = (jnp.dot(xh, wr_ref[0], preferred_element_type=F32)
              + jnp.dot(xl, wr_ref[0], preferred_element_type=F32)
              + jnp.dot(xh, wr_ref[1], preferred_element_type=F32))
    lane = lax.broadcasted_iota(jnp.int32, logits.shape, 1)
    lg = jnp.where(lane < n_experts, logits, -jnp.inf)
    m1 = jnp.max(lg, axis=-1, keepdims=True)
    i1 = jnp.min(jnp.where(lg == m1, lane, LANES), axis=-1, keepdims=True)
    lg2 = jnp.where(lane == i1, -jnp.inf, lg)
    m2 = jnp.max(lg2, axis=-1, keepdims=True)
    i2 = jnp.min(jnp.where(lg2 == m2, lane, LANES), axis=-1, keepdims=True)
    e = jnp.exp(m2 - m1)
    w1 = 1.0 / (1.0 + e)
    w2 = e / (1.0 + e)

    sel1 = lane == i1
    sel2 = lane == i2
    onehot = jnp.where(sel1 | sel2, 1.0, 0.0)
    r = lax.broadcasted_iota(jnp.int32, (tm, tm), 0)
    c = lax.broadcasted_iota(jnp.int32, (tm, tm), 1)
    strict_lower = jnp.where(c < r, 1.0, 0.0).astype(BF16)
    before = jnp.dot(strict_lower, onehot.astype(BF16), preferred_element_type=F32) + carry_ref[0:1, :]
    rank1 = jnp.sum(jnp.where(sel1, before, 0.0), axis=-1, keepdims=True)
    rank2 = jnp.sum(jnp.where(sel2, before, 0.0), axis=-1, keepdims=True)
    total = carry_ref[0:1, :] + jnp.sum(onehot, axis=0, keepdims=True)
    carry_ref[...] = jnp.broadcast_to(total, carry_ref.shape)
    cnt_ref[...] = jnp.broadcast_to(total, cnt_ref.shape)

    info = jnp.where(lane == 0, i1.astype(F32), 0.0)
    info = jnp.where(lane == 1, i2.astype(F32), info)
    info = jnp.where(lane == 2, w1, info)
    info = jnp.where(lane == 3, w2, info)
    info = jnp.where(lane == 4, rank1, info)
    info = jnp.where(lane == 5, rank2, info)
    info_ref[...] = info


def _pool_router(h, S, g_mix, pool_w, pool_b, pool_scale, g_ffn, w_router_pad, n_experts):
    T, D = h.shape
    tm = ROW_TILE
    G, gw, _ = pool_w.shape
    halo_per_tile = tm // POOL_HALO
    row = pl.BlockSpec((tm, D), lambda i: (i, 0))
    return pl.pallas_call(
        functools.partial(_pool_router_kernel, blocks_per_seq=S // tm, n_experts=n_experts),
        grid=(T // tm,),
        in_specs=[
            row,
            pl.BlockSpec((POOL_HALO, D), lambda i: (jnp.maximum(i * halo_per_tile - 1, 0), 0)),
            _const_spec((1, D)), _const_spec((G, gw, gw)), _const_spec((1, D)), _const_spec((1, D)),
            _const_spec((1, D)), _const_spec((2, D, LANES)),
        ],
        out_specs=[row, row,
                   pl.BlockSpec((tm, LANES), lambda i: (i, 0)),
                   pl.BlockSpec((8, LANES), lambda i: (0, 0))],
        out_shape=[jax.ShapeDtypeStruct((T, D), F32), jax.ShapeDtypeStruct((T, D), F32),
                   jax.ShapeDtypeStruct((T, LANES), F32), jax.ShapeDtypeStruct((8, LANES), F32)],
        scratch_shapes=[pltpu.VMEM((tm + POOL_HALO, D), F32), pltpu.VMEM((8, LANES), F32)],
        compiler_params=_params("arbitrary"),
        name="pool_router",
    )(h, h, g_mix, pool_w, pool_b, pool_scale, g_ffn, w_router_pad)


def _scatter_kernel(pos_ref, pad_ref, x_ref, out_ref, sem):
    n = x_ref.shape[0]

    def copy(src_row, n_rows, dst_row):
        return pltpu.make_async_copy(x_ref.at[pl.ds(src_row, n_rows)], out_ref.at[pl.ds(dst_row, n_rows)], sem)

    def issue(r, carry):
        for k in range(TOP_K):
            copy(r, 1, pos_ref[0, k, r]).start()
        return carry

    lax.fori_loop(0, n, issue, 0, unroll=4)

    @pl.when(pl.program_id(0) == pl.num_programs(0) - 1)
    def _():
        for e in range(pad_ref.shape[1]):
            start = pad_ref[0, e]
            count = pad_ref[1, e]

            def fill(r, carry):
                copy(0, 1, start + r).start()
                return carry

            lax.fori_loop(0, count, fill, 0)

            whole = pl.multiple_of(count // 8 * 8, 8)

            @pl.when(whole > 0)
            def _():
                copy(0, whole, 0).wait()

            def drain(r, carry):
                copy(0, 1, 0).wait()
                return carry

            lax.fori_loop(0, count - whole, drain, 0)

        tile = EXPERT_TILE
        n_tiles = out_ref.shape[0] // tile
        for k in range(pad_ref.shape[1]):
            t_idx = pad_ref[2, 0] + k

            @pl.when(t_idx < n_tiles)
            def _():
                c = copy(0, tile, pl.multiple_of(t_idx * tile, tile))
                c.start()
                c.wait()

    for k in range(TOP_K):
        copy(0, n, 0).wait()


def _row_scatter(x, pos, pad, n_out):
    T, D = x.shape
    n = GATHER_ROWS
    pos_blocks = pos.reshape(T // n, n, TOP_K).transpose(0, 2, 1)
    return pl.pallas_call(
        _scatter_kernel,
        grid=(T // n,),
        in_specs=[pl.BlockSpec((1, TOP_K, n), lambda i: (i, 0, 0), memory_space=pltpu.SMEM),
                  pl.BlockSpec(memory_space=pltpu.SMEM),
                  pl.BlockSpec((n, D), lambda i: (i, 0))],
        out_specs=pl.BlockSpec(memory_space=pl.ANY),
        out_shape=jax.ShapeDtypeStruct((n_out, D), x.dtype),
        scratch_shapes=[pltpu.SemaphoreType.DMA],
        compiler_params=_params("arbitrary"),
        name="row_scatter",
    )(pos_blocks, pad, x)


def _gather_kernel(idx_ref, src_ref, out_ref, sem):
    n = out_ref.shape[0]

    def issue(r, carry):
        pltpu.make_async_copy(src_ref.at[pl.ds(idx_ref[0, 0, r], 1)], out_ref.at[pl.ds(r, 1)], sem).start()
        return carry

    lax.fori_loop(0, n, issue, 0, unroll=8)
    pltpu.make_async_copy(src_ref.at[pl.ds(0, n)], out_ref, sem).wait()


def _row_gather(src, idx):
    n_out = idx.shape[0]
    n = GATHER_ROWS
    idx3 = idx.reshape(n_out // n, 1, n)
    tail = (0,) * (src.ndim - 1)
    return pl.pallas_call(
        _gather_kernel,
        grid=(n_out // n,),
        in_specs=[pl.BlockSpec((1, 1, n), lambda i: (i, 0, 0), memory_space=pltpu.SMEM),
                  pl.BlockSpec(memory_space=pl.ANY)],
        out_specs=pl.BlockSpec((n,) + src.shape[1:], lambda i: (i,) + tail),
        out_shape=jax.ShapeDtypeStruct((n_out,) + src.shape[1:], src.dtype),
        scratch_shapes=[pltpu.SemaphoreType.DMA],
        compiler_params=_params("arbitrary"),
        name="row_gather",
    )(idx3, src)


def _expert_kernel(te_ref, nt_ref, x_ref, wg_ref, wu_ref, wd_ref, o_ref, acc_ref, xb_ref):
    i = pl.program_id(0)
    j = pl.program_id(1)
    last = pl.num_programs(1) - 1
    used = i < nt_ref[0]

    @pl.when(used & (j == 0))
    def _():
        xb_ref[...] = x_ref[...].astype(BF16)

    @pl.when(used)
    def _():
        x = xb_ref[...]
        g = jnp.dot(x, wg_ref[0], preferred_element_type=F32)
        u = jnp.dot(x, wu_ref[0], preferred_element_type=F32)
        act = (g * jax.nn.sigmoid(g) * u).astype(BF16)
        part = jnp.dot(act, wd_ref[0], preferred_element_type=F32)

        @pl.when(j == 0)
        def _():
            acc_ref[...] = part

        @pl.when(j > 0)
        def _():
            acc_ref[...] += part

        @pl.when(j == last)
        def _():
            o_ref[...] = acc_ref[...]

    @pl.when(jnp.logical_not(used) & (j == last))
    def _():
        o_ref[...] = jnp.zeros_like(o_ref)


def _experts(xs, tile_expert, n_tiles_used, w_gate, w_up, w_down):
    P = xs.shape[0]
    E, D, F = w_gate.shape
    tm, tf = EXPERT_TILE, EXPERT_F_TILE
    n_f = F // tf

    def row_map(i, j, te, nt):
        return (jnp.minimum(i, nt[0] - 1), 0)

    def col_w(i, j, te, nt):
        used = i < nt[0]
        return (te[i], 0, jnp.where(used, j, n_f - 1))

    def row_w(i, j, te, nt):
        used = i < nt[0]
        return (te[i], jnp.where(used, j, n_f - 1), 0)

    grid_spec = pltpu.PrefetchScalarGridSpec(
        num_scalar_prefetch=2,
        grid=(P // tm, n_f),
        in_specs=[pl.BlockSpec((tm, D), row_map),
                  pl.BlockSpec((1, D, tf), col_w),
                  pl.BlockSpec((1, D, tf), col_w),
                  pl.BlockSpec((1, tf, D), row_w)],
        out_specs=pl.BlockSpec((tm, D), lambda i, j, te, nt: (i, 0)),
        scratch_shapes=[pltpu.VMEM((tm, D), F32), pltpu.VMEM((tm, D), BF16)],
    )
    return pl.pallas_call(
        _expert_kernel,
        grid_spec=grid_spec,
        out_shape=jax.ShapeDtypeStruct((P, D), F32),
        compiler_params=_params("arbitrary", "arbitrary"),
        name="expert_swiglu",
    )(tile_expert, n_tiles_used, xs, w_gate, w_up, w_down)


def _combine_kernel(h_ref, y1_ref, y2_ref, info_ref, p_ref, gp_ref, wpg_ref, wpp_ref, o_ref):
    info = info_ref[...]
    h = h_ref[...] + info[:, 2:3] * y1_ref[...] + info[:, 3:4] * y2_ref[...]
    o_ref[...] = _ple(h, p_ref[...], gp_ref[...], wpg_ref[...], wpp_ref[...])


def _combine(h, yk, info, p3, layer, g_ple, w_pg, w_pp):
    T, D = h.shape
    PD = p3.shape[2]
    tm = ROW_TILE
    nb = T // tm
    row = lambda w: pl.BlockSpec((tm, w), lambda i: (i, 0))
    return pl.pallas_call(
        _combine_kernel,
        grid=(nb,),
        in_specs=[row(D), row(D), pl.BlockSpec((tm, D), lambda i: (nb + i, 0)), row(LANES),
                  pl.BlockSpec((None, tm, PD), lambda i: (layer, i, 0)),
                  _const_spec((1, D)), _const_spec((D, D)), _const_spec((PD, D))],
        out_specs=row(D),
        out_shape=jax.ShapeDtypeStruct((T, D), F32),
        compiler_params=_params("parallel"),
        name="combine_ple",
    )(h, yk, yk, info, p3, g_ple, w_pg, w_pp)


def _rope_tables(S):
    inv = 1.0 / (ROPE_THETA ** (jnp.arange(0, DH, 2, dtype=F32) / DH))
    ang = jnp.arange(S, dtype=F32)[:, None] * inv[None, :]
    cos, sin = jnp.cos(ang), jnp.sin(ang)
    reps = HEAD_W // DH
    cos_t = jnp.tile(jnp.concatenate([cos, cos], axis=1), (1, reps))
    sin_t = jnp.tile(jnp.concatenate([-sin, sin], axis=1), (1, reps))
    return cos_t, sin_t


def _routing_tables(info, counts_f, n_experts, T):
    tm = EXPERT_TILE
    e12 = info[:, 0:2].astype(jnp.int32)
    rank = info[:, 4:6].astype(jnp.int32)
    counts = counts_f[0, :n_experts].astype(jnp.int32)
    padded = (counts + tm - 1) // tm * tm
    ends = jnp.cumsum(padded)
    starts = ends - padded
    onehot = e12[:, :, None] == jnp.arange(n_experts, dtype=jnp.int32)
    pos = jnp.sum(jnp.where(onehot, starts, 0), axis=-1) + rank
    P = T * TOP_K + n_experts * tm
    n_tiles_used = (ends[-1] // tm).reshape(1)
    pad = jnp.stack([starts + counts, padded - counts, jnp.broadcast_to(n_tiles_used, (n_experts,))])
    tile_start = jnp.arange(P // tm, dtype=jnp.int32) * tm
    tile_expert = jnp.minimum(jnp.sum(tile_start[:, None] >= ends[None, :], axis=1), n_experts - 1)
    return pos, pad, P, tile_expert.astype(jnp.int32), n_tiles_used.astype(jnp.int32)


def kernel(x, p, norm_mix, norm_ffn, norm_ple, attn_w_qkv, attn_w_o, attn_q_norm, attn_k_norm,
           attn_lambda_q1, attn_lambda_k1, attn_lambda_q2, attn_lambda_k2, attn_subln,
           pool_w, pool_b, pool_scale, ffn_w_gate, ffn_w_up, ffn_w_down,
           moe_router, moe_w_gate, moe_w_up, moe_w_down, ple_w_proj, ple_w_gate):
    B, S, D = x.shape
    T = B * S
    n_experts = moe_router.shape[-1]
    x2 = x.reshape(T, D)
    p3 = p.reshape(p.shape[0], T, p.shape[-1])
    bf = lambda a: a.astype(BF16)
    vec = lambda a: a.reshape(1, -1)

    lam_init = 0.8 - 0.6 * math.exp(-0.3 * 0)
    cos_t, sin_t = _rope_tables(S)
    reps = HEAD_W // DH
    qk_gain = jnp.zeros((8, LANES), F32)
    qk_gain = qk_gain.at[0].set(jnp.tile(attn_q_norm[0], reps) * (ATTN_SCALE * math.log2(math.e)))
    qk_gain = qk_gain.at[1].set(jnp.tile(attn_k_norm[0], reps))
    group = jnp.arange(2 * LANES) // DH
    ones = (group[:, None] == group[None, :]).astype(BF16)
    qkv = _qkv(x2, vec(norm_mix[0]), bf(attn_w_qkv[0]), cos_t, sin_t, qk_gain, ones)
    lam_params = jnp.stack([attn_lambda_q1[0], attn_lambda_k1[0], attn_lambda_q2[0], attn_lambda_k2[0]])
    ao = _attention(qkv.reshape(B, S, 3 * D), lam_params, vec(attn_subln[0]), lam_init)
    h = _dense_tail(ao.reshape(T, D), x2, p3, 0, bf(attn_w_o[0]), vec(norm_ffn[0]),
                    bf(ffn_w_gate[0]), bf(ffn_w_up[0]), bf(ffn_w_down[0]),
                    vec(norm_ple[0]), bf(ple_w_gate[0]), bf(ple_w_proj[0]))

    w_router = jnp.zeros((D, LANES), F32).at[:, :n_experts].set(moe_router[0])
    w_router_hi = w_router.astype(BF16)
    w_router_pad = jnp.stack([w_router_hi, (w_router - w_router_hi.astype(F32)).astype(BF16)])
    h, xn, info, counts = _pool_router(h, S, vec(norm_mix[1]), bf(pool_w[0]), vec(pool_b[0]),
                                       vec(pool_scale[0]), vec(norm_ffn[1]), w_router_pad, n_experts)
    pos, pad, n_sorted, tile_expert, n_tiles_used = _routing_tables(info, counts, n_experts, T)
    xs = _row_scatter(xn, pos, pad, n_sorted)
    ys = _experts(xs, tile_expert, n_tiles_used, bf(moe_w_gate[0]), bf(moe_w_up[0]), bf(moe_w_down[0]))
    yk = _row_gather(ys, pos.T.reshape(-1))
    out = _combine(h, yk, info, p3, 1, vec(norm_ple[1]), bf(ple_w_gate[1]), bf(ple_w_proj[1]))
    return out.reshape(B, S, D)
```

```python
import functools
import math

import jax
import jax.numpy as jnp
from jax import lax
from jax.experimental import pallas as pl
from jax.experimental.pallas import tpu as pltpu

F32 = jnp.float32
BF16 = jnp.bfloat16

DH = 64
HEAD_W = 2 * DH
ATTN_SCALE = 1.0 / math.sqrt(DH)
ROPE_THETA = 10000.0
POOL_WINDOWS = (2, 4, 8, 16)
POOL_HALO = 16
TOP_K = 2
RMS_EPS = 1e-6
LANES = 128
VMEM_LIMIT = 56 * 1024 * 1024

ROW_TILE = 512
Q_TILE = 256
EXPERT_TILE = 1024
EXPERT_SUB = 512
EXPERT_F_TILE = 512
GATHER_ROWS = 1024


def _rms(x, gain):
    return x * lax.rsqrt(jnp.mean(x * x, axis=-1, keepdims=True) + RMS_EPS) * gain


def _const_spec(shape):
    zeros = (0,) * len(shape)
    return pl.BlockSpec(shape, lambda *_: zeros, pipeline_mode=pl.Buffered(1))


def _params(*sem):
    return pltpu.CompilerParams(dimension_semantics=sem, vmem_limit_bytes=VMEM_LIMIT)


def _qkv_kernel(x_ref, g_ref, w_ref, cos_ref, sin_ref, qkg_ref, ones_ref, o_ref, wb_ref):
    D = x_ref.shape[1]

    @pl.when(pl.program_id(0) == 0)
    def _():
        wb_ref[...] = w_ref[...].astype(BF16)

    xn = _rms(x_ref[...], g_ref[...]).astype(BF16)
    cos = cos_ref[...]
    sin = sin_ref[...]
    lane = lax.broadcasted_iota(jnp.int32, cos.shape, 1)
    first_half = (lane % DH) < (DH // 2)
    ones = ones_ref[...]
    width = ones.shape[0]
    for part in range(3):
        y = jnp.dot(xn, wb_ref[:, part * D:(part + 1) * D], preferred_element_type=F32)
        if part == 2:
            o_ref[:, part * D:(part + 1) * D] = y.astype(BF16)
            continue
        gain = qkg_ref[part:part + 1, :]
        for c in range(0, D, width):
            yc = y[:, c:c + width]
            yy = yc * yc
            hi = yy.astype(BF16)
            lo = (yy - hi.astype(F32)).astype(BF16)
            ss = (jnp.dot(hi, ones, preferred_element_type=F32)
                  + jnp.dot(lo, ones, preferred_element_type=F32))
            yn = yc * lax.rsqrt(ss * (1.0 / DH) + RMS_EPS)
            for s in range(0, width, LANES):
                z = yn[:, s:s + LANES] * gain
                rot = jnp.where(first_half,
                                pltpu.roll(z, LANES - DH // 2, 1),
                                pltpu.roll(z, DH // 2, 1))
                col = part * D + c + s
                o_ref[:, col:col + LANES] = (z * cos + rot * sin).astype(BF16)


def _qkv(x2, gain, w_qkv, cos_t, sin_t, qk_gain, ones):
    T, D = x2.shape
    S = cos_t.shape[0]
    tm = ROW_TILE
    blocks_per_seq = S // tm
    return pl.pallas_call(
        _qkv_kernel,
        grid=(T // tm,),
        in_specs=[
            pl.BlockSpec((tm, D), lambda i: (i, 0)),
            _const_spec((1, D)),
            _const_spec((D, 3 * D)),
            pl.BlockSpec((tm, LANES), lambda i: (i % blocks_per_seq, 0)),
            pl.BlockSpec((tm, LANES), lambda i: (i % blocks_per_seq, 0)),
            _const_spec(qk_gain.shape),
            _const_spec(ones.shape),
        ],
        out_specs=pl.BlockSpec((tm, 3 * D), lambda i: (i, 0)),
        out_shape=jax.ShapeDtypeStruct((T, 3 * D), BF16),
        scratch_shapes=[pltpu.VMEM((D, 3 * D), BF16)],
        compiler_params=_params("arbitrary"),
        name="qkv_rope",
    )(x2, gain, w_qkv, cos_t, sin_t, qk_gain, ones)


def _dot_nt(a, b):
    return lax.dot_general(a, b, (((1,), (1,)), ((), ())), preferred_element_type=F32)


def _attn_kernel(q_ref, k_ref, v_ref, lam_ref, g_ref, o_ref, vext_ref, *, lam_init):
    lp = lam_ref[...]
    lam = (jnp.exp(jnp.sum(lp[0:1] * lp[1:2], axis=-1, keepdims=True))
           - jnp.exp(jnp.sum(lp[2:3] * lp[3:4], axis=-1, keepdims=True)) + lam_init)
    S = q_ref.shape[1]
    tq = Q_TILE
    vlane = lax.broadcasted_iota(jnp.int32, (S, HEAD_W), 1)
    vext_ref[:, :HEAD_W] = v_ref[0]
    vext_ref[:, HEAD_W:] = jnp.where(vlane == 0, 1.0, 0.0).astype(BF16)
    lane = lax.broadcasted_iota(jnp.int32, (tq, HEAD_W), 1)
    row = lax.broadcasted_iota(jnp.int32, (tq, tq), 0)
    col = lax.broadcasted_iota(jnp.int32, (tq, tq), 1)
    future = col > row
    gain = g_ref[...]
    for i in range(S // tq):
        q = q_ref[0, i * tq:(i + 1) * tq, :]
        kv = (i + 1) * tq
        qs = (jnp.where(lane < DH, q, jnp.zeros_like(q)), jnp.where(lane >= DH, q, jnp.zeros_like(q)))
        acc = []
        for qc in qs:
            s_diag = jnp.where(future, -jnp.inf, _dot_nt(qc, k_ref[0, i * tq:kv, :]))
            m = jnp.max(s_diag, axis=-1, keepdims=True)
            if i > 0:
                s_off = _dot_nt(qc, k_ref[0, :i * tq, :])
                m = jnp.maximum(m, jnp.max(s_off, axis=-1, keepdims=True))
                p = jnp.concatenate([jnp.exp2(s_off - m).astype(BF16),
                                     jnp.exp2(s_diag - m).astype(BF16)], axis=1)
            else:
                p = jnp.exp2(s_diag - m).astype(BF16)
            acc.append(jnp.dot(p, vext_ref[:kv, :], preferred_element_type=F32))
        o = (acc[0][:, :HEAD_W] * (1.0 / acc[0][:, HEAD_W:HEAD_W + 1])
             - acc[1][:, :HEAD_W] * (lam / acc[1][:, HEAD_W:HEAD_W + 1]))
        o = _rms(o, gain) * (1.0 - lam_init)
        o_ref[0, i * tq:(i + 1) * tq, :] = o.astype(BF16)


def _attention(qkv3, lam_params, subln, lam_init):
    B, S, W3 = qkv3.shape
    H = W3 // 3 // HEAD_W
    blk = (1, S, HEAD_W)
    return pl.pallas_call(
        functools.partial(_attn_kernel, lam_init=lam_init),
        grid=(B, H),
        in_specs=[
            pl.BlockSpec(blk, lambda b, h: (b, 0, h)),
            pl.BlockSpec(blk, lambda b, h: (b, 0, H + h)),
            pl.BlockSpec(blk, lambda b, h: (b, 0, 2 * H + h)),
            _const_spec(lam_params.shape),
            _const_spec(subln.shape),
        ],
        out_specs=pl.BlockSpec(blk, lambda b, h: (b, 0, h)),
        out_shape=jax.ShapeDtypeStruct((B, S, H * HEAD_W), BF16),
        scratch_shapes=[pltpu.VMEM((S, 2 * HEAD_W), BF16)],
        compiler_params=_params("parallel", "parallel"),
        name="diff_attn",
    )(qkv3, qkv3, qkv3, lam_params, subln)


def _ple(h, p, g_ple, w_gate, w_proj):
    xn = _rms(h, g_ple).astype(BF16)
    gate = jax.nn.sigmoid(jnp.dot(xn, w_gate, preferred_element_type=F32))
    proj = jnp.dot(p.astype(BF16), w_proj, preferred_element_type=F32)
    return h + gate * proj


def _dense_tail_kernel(ao_ref, x_ref, p_ref, wo_ref, gf_ref, wg_ref, wu_ref, wd_ref,
                       gp_ref, wpg_ref, wpp_ref, o_ref, *, f_chunks):
    h = x_ref[...] + jnp.dot(ao_ref[...], wo_ref[...], preferred_element_type=F32)
    xn = _rms(h, gf_ref[...]).astype(BF16)
    acc = h
    for lo, hi in f_chunks:
        g = jnp.dot(xn, wg_ref[:, lo:hi], preferred_element_type=F32)
        u = jnp.dot(xn, wu_ref[:, lo:hi], preferred_element_type=F32)
        act = (g * jax.nn.sigmoid(g) * u).astype(BF16)
        acc = acc + jnp.dot(act, wd_ref[lo:hi, :], preferred_element_type=F32)
    o_ref[...] = _ple(acc, p_ref[...], gp_ref[...], wpg_ref[...], wpp_ref[...])


def _dense_tail(ao, x2, p3, layer, w_o, g_ffn, w_gate, w_up, w_down, g_ple, w_pg, w_pp):
    T, D = x2.shape
    F = w_gate.shape[1]
    PD = p3.shape[2]
    tm = ROW_TILE
    f_chunks = tuple((lo, min(lo + 1024, F)) for lo in range(0, F, 1024))
    row = lambda w: pl.BlockSpec((tm, w), lambda i: (i, 0))
    return pl.pallas_call(
        functools.partial(_dense_tail_kernel, f_chunks=f_chunks),
        grid=(T // tm,),
        in_specs=[row(D), row(D), pl.BlockSpec((None, tm, PD), lambda i: (layer, i, 0)),
                  _const_spec((D, D)), _const_spec((1, D)),
                  _const_spec((D, F)), _const_spec((D, F)), _const_spec((F, D)),
                  _const_spec((1, D)), _const_spec((D, D)), _const_spec((PD, D))],
        out_specs=row(D),
        out_shape=jax.ShapeDtypeStruct((T, D), F32),
        compiler_params=_params("parallel"),
        name="dense_tail",
    )(ao, x2, p3, w_o, g_ffn, w_gate, w_up, w_down, g_ple, w_pg, w_pp)


def _pool_router_kernel(h_ref, halo_ref, gm_ref, pw_ref, pb_ref, ps_ref, gf_ref, wr_ref,
                        h_out_ref, xn_out_ref, info_ref, cnt_ref, xbuf_ref, carry_ref,
                        *, blocks_per_seq, n_experts):
    i = pl.program_id(0)
    tm, D = h_ref.shape
    gw = D // len(POOL_WINDOWS)
    seq_block = i % blocks_per_seq

    @pl.when(i == 0)
    def _():
        carry_ref[...] = jnp.zeros_like(carry_ref)

    h = h_ref[...]
    gm = gm_ref[...]
    xbuf_ref[POOL_HALO:, :] = _rms(h, gm)

    @pl.when(seq_block == 0)
    def _():
        xbuf_ref[:POOL_HALO, :] = jnp.zeros((POOL_HALO, D), F32)

    @pl.when(seq_block != 0)
    def _():
        xbuf_ref[:POOL_HALO, :] = _rms(halo_ref[...], gm)

    t = seq_block * tm + lax.broadcasted_iota(jnp.int32, (tm, 1), 0)
    pieces = []
    for g, w in enumerate(POOL_WINDOWS):
        cs = slice(g * gw, (g + 1) * gw)
        win = xbuf_ref[POOL_HALO:, cs]
        tok = win
        for k in range(1, w):
            win = win + xbuf_ref[POOL_HALO - k:POOL_HALO - k + tm, cs]
        cnt = jnp.minimum(t + 1, w).astype(F32)
        y = (win / cnt - tok).astype(BF16)
        pieces.append(jnp.dot(y, pw_ref[g], preferred_element_type=F32))
    mix = (jnp.concatenate(pieces, axis=1) + pb_ref[...]) * ps_ref[...]
    h2 = h + mix
    h_out_ref[...] = h2
    xn = _rms(h2, gf_ref[...])
    xn_out_ref[...] = xn

    xh = xn.astype(BF16)
    xl = (xn - xh.astype(F32)).astype(BF16)
    logits = (jnp.dot(xh, wr_ref[0], preferred_element_type=F32)
              + jnp.dot(xl, wr_ref[0], preferred_element_type=F32)
              + jnp.dot(xh, wr_ref[1], preferred_element_type=F32))
    lane = lax.broadcasted_iota(jnp.int32, logits.shape, 1)
    lg = jnp.where(lane < n_experts, logits, -jnp.inf)
    m1 = jnp.max(lg, axis=-1, keepdims=True)
    i1 = jnp.min(jnp.where(lg == m1, lane, LANES), axis=-1, keepdims=True)
    lg2 = jnp.where(lane == i1, -jnp.inf, lg)
    m2 = jnp.max(lg2, axis=-1, keepdims=True)
    i2 = jnp.min(jnp.where(lg2 == m2, lane, LANES), axis=-1, keepdims=True)
    e = jnp.exp(m2 - m1)
    w1 = 1.0 / (1.0 + e)
    w2 = e / (1.0 + e)

    sel1 = lane == i1
    sel2 = lane == i2
    onehot = jnp.where(sel1 | sel2, 1.0, 0.0)
    r = lax.broadcasted_iota(jnp.int32, (tm, tm), 0)
    c = lax.broadcasted_iota(jnp.int32, (tm, tm), 1)
    strict_lower = jnp.where(c < r, 1.0, 0.0).astype(BF16)
    before = jnp.dot(strict_lower, onehot.astype(BF16), preferred_element_type=F32) + carry_ref[0:1, :]
    rank1 = jnp.sum(jnp.where(sel1, before, 0.0), axis=-1, keepdims=True)
    rank2 = jnp.sum(jnp.where(sel2, before, 0.0), axis=-1, keepdims=True)
    total = carry_ref[0:1, :] + jnp.sum(onehot, axis=0, keepdims=True)
    carry_ref[...] = jnp.broadcast_to(total, carry_ref.shape)
    cnt_ref[...] = jnp.broadcast_to(total, cnt_ref.shape)

    info = jnp.where(lane == 0, i1.astype(F32), 0.0)
    info = jnp.where(lane == 1, i2.astype(F32), info)
    info = jnp.where(lane == 2, w1, info)
    info = jnp.where(lane == 3, w2, info)
    info = jnp.where(lane == 4, rank1, info)
    info = jnp.where(lane == 5, rank2, info)
    info_ref[...] = info


def _pool_router(h, S, g_mix, pool_w, pool_b, pool_scale, g_ffn, w_router_pad, n_experts):
    T, D = h.shape
    tm = ROW_TILE
    G, gw, _ = pool_w.shape
    halo_per_tile = tm // POOL_HALO
    row = pl.BlockSpec((tm, D), lambda i: (i, 0))
    return pl.pallas_call(
        functools.partial(_pool_router_kernel, blocks_per_seq=S // tm, n_experts=n_experts),
        grid=(T // tm,),
        in_specs=[
            row,
            pl.BlockSpec((POOL_HALO, D), lambda i: (jnp.maximum(i * halo_per_tile - 1, 0), 0)),
            _const_spec((1, D)), _const_spec((G, gw, gw)), _const_spec((1, D)), _const_spec((1, D)),
            _const_spec((1, D)), _const_spec((2, D, LANES)),
        ],
        out_specs=[row, row,
                   pl.BlockSpec((tm, LANES), lambda i: (i, 0)),
                   pl.BlockSpec((8, LANES), lambda i: (0, 0))],
        out_shape=[jax.ShapeDtypeStruct((T, D), F32), jax.ShapeDtypeStruct((T, D), F32),
                   jax.ShapeDtypeStruct((T, LANES), F32), jax.ShapeDtypeStruct((8, LANES), F32)],
        scratch_shapes=[pltpu.VMEM((tm + POOL_HALO, D), F32), pltpu.VMEM((8, LANES), F32)],
        compiler_params=_params("arbitrary"),
        name="pool_router",
    )(h, h, g_mix, pool_w, pool_b, pool_scale, g_ffn, w_router_pad)


def _scatter_kernel(pos_ref, pad_ref, blk_ref, x_ref, out_ref, sem):
    n = x_ref.shape[0]

    def copy(src_row, n_rows, dst_row):
        return pltpu.make_async_copy(x_ref.at[pl.ds(src_row, n_rows)], out_ref.at[pl.ds(dst_row, n_rows)], sem)

    def issue(r, carry):
        for k in range(TOP_K):
            copy(r, 1, pos_ref[0, k, r]).start()
        return carry

    lax.fori_loop(0, n, issue, 0, unroll=4)

    @pl.when(pl.program_id(0) == pl.num_programs(0) - 1)
    def _():
        for e in range(pad_ref.shape[1]):
            start = pad_ref[0, e]
            count = pad_ref[1, e]

            def fill(r, carry):
                copy(0, 1, start + r).start()
                return carry

            lax.fori_loop(0, count, fill, 0)

            whole = pl.multiple_of(count // 8 * 8, 8)

            @pl.when(whole > 0)
            def _():
                copy(0, whole, 0).wait()

            def drain(r, carry):
                copy(0, 1, 0).wait()
                return carry

            lax.fori_loop(0, count - whole, drain, 0)

        for k in range(blk_ref.shape[0]):
            blk = blk_ref[k]

            @pl.when(blk >= 0)
            def _():
                c = copy(0, EXPERT_SUB, pl.multiple_of(blk * EXPERT_SUB, EXPERT_SUB))
                c.start()
                c.wait()

    for k in range(TOP_K):
        copy(0, n, 0).wait()


def _row_scatter(x, pos, pad, empty_blocks, n_out):
    T, D = x.shape
    n = GATHER_ROWS
    assert EXPERT_SUB <= n
    pos_blocks = pos.reshape(T // n, n, TOP_K).transpose(0, 2, 1)
    return pl.pallas_call(
        _scatter_kernel,
        grid=(T // n,),
        in_specs=[pl.BlockSpec((1, TOP_K, n), lambda i: (i, 0, 0), memory_space=pltpu.SMEM),
                  pl.BlockSpec(memory_space=pltpu.SMEM),
                  pl.BlockSpec(memory_space=pltpu.SMEM),
                  pl.BlockSpec((n, D), lambda i: (i, 0))],
        out_specs=pl.BlockSpec(memory_space=pl.ANY),
        out_shape=jax.ShapeDtypeStruct((n_out, D), x.dtype),
        scratch_shapes=[pltpu.SemaphoreType.DMA],
        compiler_params=_params("arbitrary"),
        name="row_scatter",
    )(pos_blocks, pad, empty_blocks, x)


def _gather_kernel(idx_ref, src_ref, out_ref, sem):
    n = out_ref.shape[0]

    def issue(r, carry):
        pltpu.make_async_copy(src_ref.at[pl.ds(idx_ref[0, 0, r], 1)], out_ref.at[pl.ds(r, 1)], sem).start()
        return carry

    lax.fori_loop(0, n, issue, 0, unroll=8)
    pltpu.make_async_copy(src_ref.at[pl.ds(0, n)], out_ref, sem).wait()


def _row_gather(src, idx):
    n_out = idx.shape[0]
    n = GATHER_ROWS
    idx3 = idx.reshape(n_out // n, 1, n)
    tail = (0,) * (src.ndim - 1)
    return pl.pallas_call(
        _gather_kernel,
        grid=(n_out // n,),
        in_specs=[pl.BlockSpec((1, 1, n), lambda i: (i, 0, 0), memory_space=pltpu.SMEM),
                  pl.BlockSpec(memory_space=pl.ANY)],
        out_specs=pl.BlockSpec((n,) + src.shape[1:], lambda i: (i,) + tail),
        out_shape=jax.ShapeDtypeStruct((n_out,) + src.shape[1:], src.dtype),
        scratch_shapes=[pltpu.SemaphoreType.DMA],
        compiler_params=_params("arbitrary"),
        name="row_gather",
    )(idx3, src)


def _expert_kernel(te_ref, tr_ref, tx_ref, x_ref, wg_ref, wu_ref, wd_ref, o_ref, acc_ref):
    i = pl.program_id(0)
    j = pl.program_id(1)
    last = pl.num_programs(1) - 1
    tm, D = x_ref.shape
    sub = EXPERT_SUB
    n_live = (tr_ref[i] + sub - 1) // sub

    for n in range(1, tm // sub + 1):
        m = n * sub

        @pl.when(n_live == n)
        def _():
            x = x_ref[:m, :].astype(BF16)
            g = jnp.dot(x, wg_ref[0].astype(BF16), preferred_element_type=F32)
            u = jnp.dot(x, wu_ref[0].astype(BF16), preferred_element_type=F32)
            act = (g * jax.nn.sigmoid(g) * u).astype(BF16)
            part = jnp.dot(act, wd_ref[0].astype(BF16), preferred_element_type=F32)

            @pl.when(j == 0)
            def _():
                acc_ref[:m, :] = part

            @pl.when(j > 0)
            def _():
                acc_ref[:m, :] += part

            @pl.when(j == last)
            def _():
                o_ref[:m, :] = acc_ref[:m, :]
                if m < tm:
                    o_ref[m:, :] = jnp.zeros((tm - m, D), o_ref.dtype)

    @pl.when((n_live == 0) & (j == last))
    def _():
        o_ref[...] = jnp.zeros_like(o_ref)


def _experts(xs, tile_expert, tile_rows, tile_src, w_gate, w_up, w_down):
    P = xs.shape[0]
    E, D, F = w_gate.shape
    tm, tf = EXPERT_TILE, EXPERT_F_TILE
    n_f = F // tf

    def row_map(i, j, te, tr, tx):
        return (tx[i], 0)

    def col_w(i, j, te, tr, tx):
        return (te[i], 0, jnp.where(tr[i] > 0, j, n_f - 1))

    def row_w(i, j, te, tr, tx):
        return (te[i], jnp.where(tr[i] > 0, j, n_f - 1), 0)

    grid_spec = pltpu.PrefetchScalarGridSpec(
        num_scalar_prefetch=3,
        grid=(P // tm, n_f),
        in_specs=[pl.BlockSpec((tm, D), row_map),
                  pl.BlockSpec((1, D, tf), col_w),
                  pl.BlockSpec((1, D, tf), col_w),
                  pl.BlockSpec((1, tf, D), row_w)],
        out_specs=pl.BlockSpec((tm, D), lambda i, j, te, tr, tx: (i, 0)),
        scratch_shapes=[pltpu.VMEM((tm, D), F32)],
    )
    return pl.pallas_call(
        _expert_kernel,
        grid_spec=grid_spec,
        out_shape=jax.ShapeDtypeStruct((P, D), F32),
        compiler_params=_params("arbitrary", "arbitrary"),
        name="expert_swiglu",
    )(tile_expert, tile_rows, tile_src, xs, w_gate, w_up, w_down)


def _combine_kernel(h_ref, y1_ref, y2_ref, info_ref, p_ref, gp_ref, wpg_ref, wpp_ref, o_ref):
    info = info_ref[...]
    h = h_ref[...] + info[:, 2:3] * y1_ref[...] + info[:, 3:4] * y2_ref[...]
    o_ref[...] = _ple(h, p_ref[...], gp_ref[...], wpg_ref[...], wpp_ref[...])


def _combine(h, yk, info, p3, layer, g_ple, w_pg, w_pp):
    T, D = h.shape
    PD = p3.shape[2]
    tm = ROW_TILE
    nb = T // tm
    row = lambda w: pl.BlockSpec((tm, w), lambda i: (i, 0))
    return pl.pallas_call(
        _combine_kernel,
        grid=(nb,),
        in_specs=[row(D), row(D), pl.BlockSpec((tm, D), lambda i: (nb + i, 0)), row(LANES),
                  pl.BlockSpec((None, tm, PD), lambda i: (layer, i, 0)),
                  _const_spec((1, D)), _const_spec((D, D)), _const_spec((PD, D))],
        out_specs=row(D),
        out_shape=jax.ShapeDtypeStruct((T, D), F32),
        compiler_params=_params("parallel"),
        name="combine_ple",
    )(h, yk, yk, info, p3, g_ple, w_pg, w_pp)


def _rope_tables(S):
    inv = 1.0 / (ROPE_THETA ** (jnp.arange(0, DH, 2, dtype=F32) / DH))
    ang = jnp.arange(S, dtype=F32)[:, None] * inv[None, :]
    cos, sin = jnp.cos(ang), jnp.sin(ang)
    reps = HEAD_W // DH
    cos_t = jnp.tile(jnp.concatenate([cos, cos], axis=1), (1, reps))
    sin_t = jnp.tile(jnp.concatenate([-sin, sin], axis=1), (1, reps))
    return cos_t, sin_t


def _routing_tables(info, counts_f, n_experts, T):
    tm = EXPERT_TILE
    e12 = info[:, 0:2].astype(jnp.int32)
    rank = info[:, 4:6].astype(jnp.int32)
    counts = counts_f[0, :n_experts].astype(jnp.int32)
    padded = (counts + tm - 1) // tm * tm
    ends = jnp.cumsum(padded)
    starts = ends - padded
    onehot = e12[:, :, None] == jnp.arange(n_experts, dtype=jnp.int32)
    pos = jnp.sum(jnp.where(onehot, starts, 0), axis=-1) + rank
    P = T * TOP_K + n_experts * tm
    sub = EXPERT_SUB
    pad = jnp.stack([starts + counts, (counts + sub - 1) // sub * sub - counts])
    blk_start = jnp.arange(P // sub, dtype=jnp.int32) * sub
    holds_rows = jnp.any((blk_start[:, None] >= starts[None, :]) & (blk_start[:, None] < (starts + counts)[None, :]), axis=1)
    slot = jnp.where(holds_rows, -1, jnp.cumsum(~holds_rows) - 1)
    hit = slot[None, :] == jnp.arange(n_experts * tm // sub, dtype=jnp.int32)[:, None]
    empty_blocks = (jnp.sum(jnp.where(hit, jnp.arange(P // sub, dtype=jnp.int32) + 1, 0), axis=1) - 1).astype(jnp.int32)
    tile_start = jnp.arange(P // tm, dtype=jnp.int32) * tm
    tile_expert = jnp.minimum(jnp.sum(tile_start[:, None] >= ends[None, :], axis=1), n_experts - 1).astype(jnp.int32)
    tile_rows = jnp.clip((starts + counts)[tile_expert] - tile_start, 0, tm).astype(jnp.int32)
    tile_src = jnp.minimum(jnp.arange(P // tm, dtype=jnp.int32), ends[-1] // tm - 1).astype(jnp.int32)
    return pos, pad, empty_blocks, P, tile_expert, tile_rows, tile_src


def kernel(x, p, norm_mix, norm_ffn, norm_ple, attn_w_qkv, attn_w_o, attn_q_norm, attn_k_norm,
           attn_lambda_q1, attn_lambda_k1, attn_lambda_q2, attn_lambda_k2, attn_subln,
           pool_w, pool_b, pool_scale, ffn_w_gate, ffn_w_up, ffn_w_down,
           moe_router, moe_w_gate, moe_w_up, moe_w_down, ple_w_proj, ple_w_gate):
    B, S, D = x.shape
    T = B * S
    n_experts = moe_router.shape[-1]
    x2 = x.reshape(T, D)
    p3 = p.reshape(p.shape[0], T, p.shape[-1])
    bf = lambda a: a.astype(BF16)
    vec = lambda a: a.reshape(1, -1)

    lam_init = 0.8 - 0.6 * math.exp(-0.3 * 0)
    cos_t, sin_t = _rope_tables(S)
    reps = HEAD_W // DH
    qk_gain = jnp.zeros((8, LANES), F32)
    qk_gain = qk_gain.at[0].set(jnp.tile(attn_q_norm[0], reps) * (ATTN_SCALE * math.log2(math.e)))
    qk_gain = qk_gain.at[1].set(jnp.tile(attn_k_norm[0], reps))
    group = jnp.arange(2 * LANES) // DH
    ones = (group[:, None] == group[None, :]).astype(BF16)
    qkv = _qkv(x2, vec(norm_mix[0]), attn_w_qkv[0], cos_t, sin_t, qk_gain, ones)
    lam_params = jnp.stack([attn_lambda_q1[0], attn_lambda_k1[0], attn_lambda_q2[0], attn_lambda_k2[0]])
    ao = _attention(qkv.reshape(B, S, 3 * D), lam_params, vec(attn_subln[0]), lam_init)
    h = _dense_tail(ao.reshape(T, D), x2, p3, 0, bf(attn_w_o[0]), vec(norm_ffn[0]),
                    bf(ffn_w_gate[0]), bf(ffn_w_up[0]), bf(ffn_w_down[0]),
                    vec(norm_ple[0]), bf(ple_w_gate[0]), bf(ple_w_proj[0]))

    w_router = jnp.zeros((D, LANES), F32).at[:, :n_experts].set(moe_router[0])
    w_router_hi = w_router.astype(BF16)
    w_router_pad = jnp.stack([w_router_hi, (w_router - w_router_hi.astype(F32)).astype(BF16)])
    h, xn, info, counts = _pool_router(h, S, vec(norm_mix[1]), bf(pool_w[0]), vec(pool_b[0]),
                                       vec(pool_scale[0]), vec(norm_ffn[1]), w_router_pad, n_experts)
    pos, pad, empty_blocks, n_sorted, tile_expert, tile_rows, tile_src = _routing_tables(info, counts, n_experts, T)
    xs = _row_scatter(xn, pos, pad, empty_blocks, n_sorted)
    ys = _experts(xs, tile_expert, tile_rows, tile_src, moe_w_gate[0], moe_w_up[0], moe_w_down[0])
    yk = _row_gather(ys, pos.T.reshape(-1))
    out = _combine(h, yk, info, p3, 1, vec(norm_ple[1]), bf(ple_w_gate[1]), bf(ple_w_proj[1]))
    return out.reshape(B, S, D)
```

```python
import functools
import math

import jax
import jax.numpy as jnp
from jax import lax
from jax.experimental import pallas as pl
from jax.experimental.pallas import tpu as pltpu

F32 = jnp.float32
BF16 = jnp.bfloat16

DH = 64
HEAD_W = 2 * DH
ATTN_SCALE = 1.0 / math.sqrt(DH)
ROPE_THETA = 10000.0
POOL_WINDOWS = (2, 4, 8, 16)
POOL_HALO = 16
TOP_K = 2
RMS_EPS = 1e-6
LANES = 128
VMEM_LIMIT = 56 * 1024 * 1024

ROW_TILE = 512
Q_TILE = 256
EXPERT_TILE = 1024
EXPERT_SUB = 512
EXPERT_F_TILE = 512
GATHER_ROWS = 1024


def _rms(x, gain):
    return x * lax.rsqrt(jnp.mean(x * x, axis=-1, keepdims=True) + RMS_EPS) * gain


def _const_spec(shape):
    zeros = (0,) * len(shape)
    return pl.BlockSpec(shape, lambda *_: zeros, pipeline_mode=pl.Buffered(1))


def _params(*sem):
    return pltpu.CompilerParams(dimension_semantics=sem, vmem_limit_bytes=VMEM_LIMIT)


def _qkv_kernel(x_ref, g_ref, w_ref, cos_ref, sin_ref, qkg_ref, ones_ref, o_ref, wb_ref):
    D = x_ref.shape[1]

    @pl.when(pl.program_id(0) == 0)
    def _():
        wb_ref[...] = w_ref[...].astype(BF16)

    xn = _rms(x_ref[...], g_ref[...]).astype(BF16)
    cos = cos_ref[...]
    sin = sin_ref[...]
    lane = lax.broadcasted_iota(jnp.int32, cos.shape, 1)
    first_half = (lane % DH) < (DH // 2)
    ones = ones_ref[...]
    width = ones.shape[0]
    for part in range(3):
        y = jnp.dot(xn, wb_ref[:, part * D:(part + 1) * D], preferred_element_type=F32)
        if part == 2:
            o_ref[:, part * D:(part + 1) * D] = y.astype(BF16)
            continue
        gain = qkg_ref[part:part + 1, :]
        for c in range(0, D, width):
            yc = y[:, c:c + width]
            yy = yc * yc
            hi = yy.astype(BF16)
            lo = (yy - hi.astype(F32)).astype(BF16)
            ss = (jnp.dot(hi, ones, preferred_element_type=F32)
                  + jnp.dot(lo, ones, preferred_element_type=F32))
            yn = yc * lax.rsqrt(ss * (1.0 / DH) + RMS_EPS)
            for s in range(0, width, LANES):
                z = yn[:, s:s + LANES] * gain
                rot = jnp.where(first_half,
                                pltpu.roll(z, LANES - DH // 2, 1),
                                pltpu.roll(z, DH // 2, 1))
                col = part * D + c + s
                o_ref[:, col:col + LANES] = (z * cos + rot * sin).astype(BF16)


def _qkv(x2, gain, w_qkv, cos_t, sin_t, qk_gain, ones):
    T, D = x2.shape
    S = cos_t.shape[0]
    tm = ROW_TILE
    blocks_per_seq = S // tm
    return pl.pallas_call(
        _qkv_kernel,
        grid=(T // tm,),
        in_specs=[
            pl.BlockSpec((tm, D), lambda i: (i, 0)),
            _const_spec((1, D)),
            _const_spec((D, 3 * D)),
            pl.BlockSpec((tm, LANES), lambda i: (i % blocks_per_seq, 0)),
            pl.BlockSpec((tm, LANES), lambda i: (i % blocks_per_seq, 0)),
            _const_spec(qk_gain.shape),
            _const_spec(ones.shape),
        ],
        out_specs=pl.BlockSpec((tm, 3 * D), lambda i: (i, 0)),
        out_shape=jax.ShapeDtypeStruct((T, 3 * D), BF16),
        scratch_shapes=[pltpu.VMEM((D, 3 * D), BF16)],
        compiler_params=_params("arbitrary"),
        name="qkv_rope",
    )(x2, gain, w_qkv, cos_t, sin_t, qk_gain, ones)


def _dot_nt(a, b):
    return lax.dot_general(a, b, (((1,), (1,)), ((), ())), preferred_element_type=F32)


def _attn_kernel(q_ref, k_ref, v_ref, lam_ref, g_ref, o_ref, vext_ref, *, lam_init):
    lp = lam_ref[...]
    lam = (jnp.exp(jnp.sum(lp[0:1] * lp[1:2], axis=-1, keepdims=True))
           - jnp.exp(jnp.sum(lp[2:3] * lp[3:4], axis=-1, keepdims=True)) + lam_init)
    S = q_ref.shape[1]
    tq = Q_TILE
    vlane = lax.broadcasted_iota(jnp.int32, (S, HEAD_W), 1)
    vext_ref[:, :HEAD_W] = v_ref[0]
    vext_ref[:, HEAD_W:] = jnp.where(vlane == 0, 1.0, 0.0).astype(BF16)
    lane = lax.broadcasted_iota(jnp.int32, (tq, HEAD_W), 1)
    row = lax.broadcasted_iota(jnp.int32, (tq, tq), 0)
    col = lax.broadcasted_iota(jnp.int32, (tq, tq), 1)
    future = col > row
    gain = g_ref[...]
    for i in range(S // tq):
        q = q_ref[0, i * tq:(i + 1) * tq, :]
        kv = (i + 1) * tq
        qs = (jnp.where(lane < DH, q, jnp.zeros_like(q)), jnp.where(lane >= DH, q, jnp.zeros_like(q)))
        acc = []
        for qc in qs:
            s_diag = jnp.where(future, -jnp.inf, _dot_nt(qc, k_ref[0, i * tq:kv, :]))
            m = jnp.max(s_diag, axis=-1, keepdims=True)
            if i > 0:
                s_off = _dot_nt(qc, k_ref[0, :i * tq, :])
                m = jnp.maximum(m, jnp.max(s_off, axis=-1, keepdims=True))
                p = jnp.concatenate([jnp.exp2(s_off - m).astype(BF16),
                                     jnp.exp2(s_diag - m).astype(BF16)], axis=1)
            else:
                p = jnp.exp2(s_diag - m).astype(BF16)
            acc.append(jnp.dot(p, vext_ref[:kv, :], preferred_element_type=F32))
        o = (acc[0][:, :HEAD_W] * (1.0 / acc[0][:, HEAD_W:HEAD_W + 1])
             - acc[1][:, :HEAD_W] * (lam / acc[1][:, HEAD_W:HEAD_W + 1]))
        o = _rms(o, gain) * (1.0 - lam_init)
        o_ref[0, i * tq:(i + 1) * tq, :] = o.astype(BF16)


def _attention(qkv3, lam_params, subln, lam_init):
    B, S, W3 = qkv3.shape
    H = W3 // 3 // HEAD_W
    blk = (1, S, HEAD_W)
    return pl.pallas_call(
        functools.partial(_attn_kernel, lam_init=lam_init),
        grid=(B, H),
        in_specs=[
            pl.BlockSpec(blk, lambda b, h: (b, 0, h)),
            pl.BlockSpec(blk, lambda b, h: (b, 0, H + h)),
            pl.BlockSpec(blk, lambda b, h: (b, 0, 2 * H + h)),
            _const_spec(lam_params.shape),
            _const_spec(subln.shape),
        ],
        out_specs=pl.BlockSpec(blk, lambda b, h: (b, 0, h)),
        out_shape=jax.ShapeDtypeStruct((B, S, H * HEAD_W), BF16),
        scratch_shapes=[pltpu.VMEM((S, 2 * HEAD_W), BF16)],
        compiler_params=_params("parallel", "parallel"),
        name="diff_attn",
    )(qkv3, qkv3, qkv3, lam_params, subln)


def _ple(h, p, g_ple, w_gate, w_proj):
    xn = _rms(h, g_ple).astype(BF16)
    gate = jax.nn.sigmoid(jnp.dot(xn, w_gate, preferred_element_type=F32))
    proj = jnp.dot(p.astype(BF16), w_proj, preferred_element_type=F32)
    return h + gate * proj


def _dense_tail_kernel(ao_ref, x_ref, p_ref, wo_ref, gf_ref, wg_ref, wu_ref, wd_ref,
                       gp_ref, wpg_ref, wpp_ref, o_ref, *, f_chunks):
    h = x_ref[...] + jnp.dot(ao_ref[...], wo_ref[...], preferred_element_type=F32)
    xn = _rms(h, gf_ref[...]).astype(BF16)
    acc = h
    for lo, hi in f_chunks:
        g = jnp.dot(xn, wg_ref[:, lo:hi], preferred_element_type=F32)
        u = jnp.dot(xn, wu_ref[:, lo:hi], preferred_element_type=F32)
        act = (g * jax.nn.sigmoid(g) * u).astype(BF16)
        acc = acc + jnp.dot(act, wd_ref[lo:hi, :], preferred_element_type=F32)
    o_ref[...] = _ple(acc, p_ref[...], gp_ref[...], wpg_ref[...], wpp_ref[...])


def _dense_tail(ao, x2, p3, layer, w_o, g_ffn, w_gate, w_up, w_down, g_ple, w_pg, w_pp):
    T, D = x2.shape
    F = w_gate.shape[1]
    PD = p3.shape[2]
    tm = ROW_TILE
    f_chunks = tuple((lo, min(lo + 1024, F)) for lo in range(0, F, 1024))
    row = lambda w: pl.BlockSpec((tm, w), lambda i: (i, 0))
    return pl.pallas_call(
        functools.partial(_dense_tail_kernel, f_chunks=f_chunks),
        grid=(T // tm,),
        in_specs=[row(D), row(D), pl.BlockSpec((None, tm, PD), lambda i: (layer, i, 0)),
                  _const_spec((D, D)), _const_spec((1, D)),
                  _const_spec((D, F)), _const_spec((D, F)), _const_spec((F, D)),
                  _const_spec((1, D)), _const_spec((D, D)), _const_spec((PD, D))],
        out_specs=row(D),
        out_shape=jax.ShapeDtypeStruct((T, D), F32),
        compiler_params=_params("parallel"),
        name="dense_tail",
    )(ao, x2, p3, w_o, g_ffn, w_gate, w_up, w_down, g_ple, w_pg, w_pp)


def _pool_router_kernel(h_ref, halo_ref, gm_ref, pw_ref, pb_ref, ps_ref, gf_ref, wr_ref,
                        h_out_ref, xn_out_ref, info_ref, cnt_ref, xbuf_ref, carry_ref,
                        *, blocks_per_seq, n_experts):
    i = pl.program_id(0)
    tm, D = h_ref.shape
    gw = D // len(POOL_WINDOWS)
    seq_block = i % blocks_per_seq

    @pl.when(i == 0)
    def _():
        carry_ref[...] = jnp.zeros_like(carry_ref)

    h = h_ref[...]
    gm = gm_ref[...]
    xbuf_ref[POOL_HALO:, :] = _rms(h, gm)

    @pl.when(seq_block == 0)
    def _():
        xbuf_ref[:POOL_HALO, :] = jnp.zeros((POOL_HALO, D), F32)

    @pl.when(seq_block != 0)
    def _():
        xbuf_ref[:POOL_HALO, :] = _rms(halo_ref[...], gm)

    t = seq_block * tm + lax.broadcasted_iota(jnp.int32, (tm, 1), 0)
    pieces = []
    for g, w in enumerate(POOL_WINDOWS):
        cs = slice(g * gw, (g + 1) * gw)
        win = xbuf_ref[POOL_HALO:, cs]
        tok = win
        for k in range(1, w):
            win = win + xbuf_ref[POOL_HALO - k:POOL_HALO - k + tm, cs]
        cnt = jnp.minimum(t + 1, w).astype(F32)
        y = (win / cnt - tok).astype(BF16)
        pieces.append(jnp.dot(y, pw_ref[g], preferred_element_type=F32))
    mix = (jnp.concatenate(pieces, axis=1) + pb_ref[...]) * ps_ref[...]
    h2 = h + mix
    h_out_ref[...] = h2
    xn = _rms(h2, gf_ref[...])
    xn_out_ref[...] = xn

    xh = xn.astype(BF16)
    xl = (xn - xh.astype(F32)).astype(BF16)
    logits = (jnp.dot(xh, wr_ref[0], preferred_element_type=F32)
              + jnp.dot(xl, wr_ref[0], preferred_element_type=F32)
              + jnp.dot(xh, wr_ref[1], preferred_element_type=F32))
    lane = lax.broadcasted_iota(jnp.int32, logits.shape, 1)
    lg = jnp.where(lane < n_experts, logits, -jnp.inf)
    m1 = jnp.max(lg, axis=-1, keepdims=True)
    i1 = jnp.min(jnp.where(lg == m1, lane, LANES), axis=-1, keepdims=True)
    lg2 = jnp.where(lane == i1, -jnp.inf, lg)
    m2 = jnp.max(lg2, axis=-1, keepdims=True)
    i2 = jnp.min(jnp.where(lg2 == m2, lane, LANES), axis=-1, keepdims=True)
    e = jnp.exp(m2 - m1)
    w1 = 1.0 / (1.0 + e)
    w2 = e / (1.0 + e)

    sel1 = lane == i1
    sel2 = lane == i2
    onehot = jnp.where(sel1 | sel2, 1.0, 0.0)
    r = lax.broadcasted_iota(jnp.int32, (tm, tm), 0)
    c = lax.broadcasted_iota(jnp.int32, (tm, tm), 1)
    strict_lower = jnp.where(c < r, 1.0, 0.0).astype(BF16)
    before = jnp.dot(strict_lower, onehot.astype(BF16), preferred_element_type=F32) + carry_ref[0:1, :]
    rank1 = jnp.sum(jnp.where(sel1, before, 0.0), axis=-1, keepdims=True)
    rank2 = jnp.sum(jnp.where(sel2, before, 0.0), axis=-1, keepdims=True)
    total = carry_ref[0:1, :] + jnp.sum(onehot, axis=0, keepdims=True)
    carry_ref[...] = jnp.broadcast_to(total, carry_ref.shape)
    cnt_ref[...] = jnp.broadcast_to(total, cnt_ref.shape)

    info = jnp.where(lane == 0, i1.astype(F32), 0.0)
    info = jnp.where(lane == 1, i2.astype(F32), info)
    info = jnp.where(lane == 2, w1, info)
    info = jnp.where(lane == 3, w2, info)
    info = jnp.where(lane == 4, rank1, info)
    info = jnp.where(lane == 5, rank2, info)
    info_ref[...] = info


def _pool_router(h, S, g_mix, pool_w, pool_b, pool_scale, g_ffn, w_router_pad, n_experts):
    T, D = h.shape
    tm = ROW_TILE
    G, gw, _ = pool_w.shape
    halo_per_tile = tm // POOL_HALO
    row = pl.BlockSpec((tm, D), lambda i: (i, 0))
    return pl.pallas_call(
        functools.partial(_pool_router_kernel, blocks_per_seq=S // tm, n_experts=n_experts),
        grid=(T // tm,),
        in_specs=[
            row,
            pl.BlockSpec((POOL_HALO, D), lambda i: (jnp.maximum(i * halo_per_tile - 1, 0), 0)),
            _const_spec((1, D)), _const_spec((G, gw, gw)), _const_spec((1, D)), _const_spec((1, D)),
            _const_spec((1, D)), _const_spec((2, D, LANES)),
        ],
        out_specs=[row, row,
                   pl.BlockSpec((tm, LANES), lambda i: (i, 0)),
                   pl.BlockSpec((8, LANES), lambda i: (0, 0))],
        out_shape=[jax.ShapeDtypeStruct((T, D), F32), jax.ShapeDtypeStruct((T, D), F32),
                   jax.ShapeDtypeStruct((T, LANES), F32), jax.ShapeDtypeStruct((8, LANES), F32)],
        scratch_shapes=[pltpu.VMEM((tm + POOL_HALO, D), F32), pltpu.VMEM((8, LANES), F32)],
        compiler_params=_params("arbitrary"),
        name="pool_router",
    )(h, h, g_mix, pool_w, pool_b, pool_scale, g_ffn, w_router_pad)


def _scatter_kernel(pos_ref, pad_ref, blk_ref, inv_init_ref, x_ref, out_ref, inv_ref, sem):
    n = x_ref.shape[0]
    n_tokens = n * pl.num_programs(0)
    first = pl.program_id(0) * n

    def copy(src_row, n_rows, dst_row):
        return pltpu.make_async_copy(x_ref.at[pl.ds(src_row, n_rows)], out_ref.at[pl.ds(dst_row, n_rows)], sem)

    def issue(r, carry):
        for k in range(TOP_K):
            p = pos_ref[0, k, r]
            copy(r, 1, p).start()
            inv_ref[p] = k * n_tokens + first + r
        return carry

    lax.fori_loop(0, n, issue, 0, unroll=4)

    @pl.when(pl.program_id(0) == pl.num_programs(0) - 1)
    def _():
        for e in range(pad_ref.shape[1]):
            start = pad_ref[0, e]
            count = pad_ref[1, e]

            def fill(r, carry):
                copy(0, 1, start + r).start()
                inv_ref[start + r] = inv_init_ref[start + r]
                return carry

            lax.fori_loop(0, count, fill, 0)

            whole = pl.multiple_of(count // 8 * 8, 8)

            @pl.when(whole > 0)
            def _():
                copy(0, whole, 0).wait()

            def drain(r, carry):
                copy(0, 1, 0).wait()
                return carry

            lax.fori_loop(0, count - whole, drain, 0)

        for k in range(blk_ref.shape[0]):
            blk = blk_ref[k]

            @pl.when(blk >= 0)
            def _():
                c = copy(0, EXPERT_SUB, pl.multiple_of(blk * EXPERT_SUB, EXPERT_SUB))
                c.start()

                def keep(r, carry):
                    inv_ref[blk * EXPERT_SUB + r] = inv_init_ref[blk * EXPERT_SUB + r]
                    return carry

                lax.fori_loop(0, EXPERT_SUB, keep, 0, unroll=8)
                c.wait()

    for k in range(TOP_K):
        copy(0, n, 0).wait()


def _row_scatter(x, pos, pad, empty_blocks, n_out):
    T, D = x.shape
    n = GATHER_ROWS
    assert EXPERT_SUB <= n
    pos_blocks = pos.reshape(T // n, n, TOP_K).transpose(0, 2, 1)
    pad_dst = TOP_K * T + jnp.arange(n_out, dtype=jnp.int32) % EXPERT_TILE
    return pl.pallas_call(
        _scatter_kernel,
        grid=(T // n,),
        in_specs=[pl.BlockSpec((1, TOP_K, n), lambda i: (i, 0, 0), memory_space=pltpu.SMEM),
                  pl.BlockSpec(memory_space=pltpu.SMEM),
                  pl.BlockSpec(memory_space=pltpu.SMEM),
                  pl.BlockSpec(memory_space=pltpu.SMEM),
                  pl.BlockSpec((n, D), lambda i: (i, 0))],
        out_specs=[pl.BlockSpec(memory_space=pl.ANY), pl.BlockSpec(memory_space=pltpu.SMEM)],
        out_shape=[jax.ShapeDtypeStruct((n_out, D), x.dtype), jax.ShapeDtypeStruct((n_out,), jnp.int32)],
        scratch_shapes=[pltpu.SemaphoreType.DMA],
        compiler_params=_params("arbitrary"),
        name="row_scatter",
    )(pos_blocks, pad, empty_blocks, pad_dst, x)


def _expert_kernel(te_ref, tr_ref, tx_ref, inv_ref, x_ref, wg_ref, wu_ref, wd_ref, yk_ref, yo_ref, sem,
                   *, n_f, dump_base):
    i = pl.program_id(0)
    j = pl.program_id(1)
    tm, D = x_ref.shape
    sub = EXPERT_SUB
    chunk = tm // n_f
    rest = tm - n_f * chunk
    slot = i % 2
    prev = 1 - slot
    n_live = (tr_ref[i] + sub - 1) // sub

    def row_copy(buf, row, dst):
        return pltpu.make_async_copy(yo_ref.at[buf, pl.ds(row, 1)], yk_ref.at[pl.ds(dst, 1)], sem.at[buf])

    def wait_rows(buf):
        pltpu.make_async_copy(yo_ref.at[buf], yk_ref.at[pl.ds(0, tm)], sem.at[buf]).wait()

    def send_prev_rows(first_row, count):
        table = jnp.maximum(i - 1, 0) * tm + first_row
        for q in range(count):
            dst = jnp.where(i == 0, dump_base + first_row + q, inv_ref[table + q])
            row_copy(prev, first_row + q, dst).start()

    @pl.when((i == 0) & (j == 0))
    def _():
        yo_ref[1] = jnp.zeros((tm, D), F32)

    @pl.when((i > 0) & (j == 0))
    def _():
        wait_rows(slot)

    for n in range(1, tm // sub + 1):
        m = n * sub

        @pl.when(n_live == n)
        def _():
            send_prev_rows(j * chunk, chunk)
            x = x_ref[:m, :].astype(BF16)
            g = jnp.dot(x, wg_ref[0].astype(BF16), preferred_element_type=F32)
            u = jnp.dot(x, wu_ref[0].astype(BF16), preferred_element_type=F32)
            act = (g * jax.nn.sigmoid(g) * u).astype(BF16)
            part = jnp.dot(act, wd_ref[0].astype(BF16), preferred_element_type=F32)

            @pl.when(j == 0)
            def _():
                send_prev_rows(n_f * chunk, rest)
                yo_ref[slot, :m, :] = part
                if m < tm:
                    yo_ref[slot, m:, :] = jnp.zeros((tm - m, D), F32)

            @pl.when(j > 0)
            def _():
                yo_ref[slot, :m, :] += part

    @pl.when(n_live == 0)
    def _():
        send_prev_rows(j * chunk, chunk)

        @pl.when(j == 0)
        def _():
            send_prev_rows(n_f * chunk, rest)
            yo_ref[slot] = jnp.zeros((tm, D), F32)

    @pl.when((i == pl.num_programs(0) - 1) & (j == n_f - 1))
    def _():
        wait_rows(prev)


def _experts(xs, inv, tile_expert, tile_rows, tile_src, w_gate, w_up, w_down, n_tokens):
    P = xs.shape[0]
    E, D, F = w_gate.shape
    tm, tf = EXPERT_TILE, EXPERT_F_TILE
    n_f = F // tf
    dump_base = TOP_K * n_tokens
    dump_rows = tm

    def row_map(i, j, te, tr, tx, inv):
        return (tx[i], 0)

    def col_w(i, j, te, tr, tx, inv):
        return (te[i], 0, jnp.where(tr[i] > 0, j, n_f - 1))

    def row_w(i, j, te, tr, tx, inv):
        return (te[i], jnp.where(tr[i] > 0, j, n_f - 1), 0)

    grid_spec = pltpu.PrefetchScalarGridSpec(
        num_scalar_prefetch=4,
        grid=(P // tm + 1, n_f),
        in_specs=[pl.BlockSpec((tm, D), row_map),
                  pl.BlockSpec((1, D, tf), col_w),
                  pl.BlockSpec((1, D, tf), col_w),
                  pl.BlockSpec((1, tf, D), row_w)],
        out_specs=pl.BlockSpec(memory_space=pl.ANY),
        scratch_shapes=[pltpu.VMEM((2, tm, D), F32), pltpu.SemaphoreType.DMA((2,))],
    )
    return pl.pallas_call(
        functools.partial(_expert_kernel, n_f=n_f, dump_base=dump_base),
        grid_spec=grid_spec,
        out_shape=jax.ShapeDtypeStruct((dump_base + dump_rows, D), F32),
        compiler_params=_params("arbitrary", "arbitrary"),
        name="expert_swiglu",
    )(tile_expert, tile_rows, tile_src, inv, xs, w_gate, w_up, w_down)


def _combine_kernel(h_ref, y1_ref, y2_ref, info_ref, p_ref, gp_ref, wpg_ref, wpp_ref, o_ref):
    info = info_ref[...]
    h = h_ref[...] + info[:, 2:3] * y1_ref[...] + info[:, 3:4] * y2_ref[...]
    o_ref[...] = _ple(h, p_ref[...], gp_ref[...], wpg_ref[...], wpp_ref[...])


def _combine(h, yk, info, p3, layer, g_ple, w_pg, w_pp):
    T, D = h.shape
    PD = p3.shape[2]
    tm = ROW_TILE
    nb = T // tm
    row = lambda w: pl.BlockSpec((tm, w), lambda i: (i, 0))
    return pl.pallas_call(
        _combine_kernel,
        grid=(nb,),
        in_specs=[row(D), row(D), pl.BlockSpec((tm, D), lambda i: (nb + i, 0)), row(LANES),
                  pl.BlockSpec((None, tm, PD), lambda i: (layer, i, 0)),
                  _const_spec((1, D)), _const_spec((D, D)), _const_spec((PD, D))],
        out_specs=row(D),
        out_shape=jax.ShapeDtypeStruct((T, D), F32),
        compiler_params=_params("parallel"),
        name="combine_ple",
    )(h, yk, yk, info, p3, g_ple, w_pg, w_pp)


def _rope_tables(S):
    inv = 1.0 / (ROPE_THETA ** (jnp.arange(0, DH, 2, dtype=F32) / DH))
    ang = jnp.arange(S, dtype=F32)[:, None] * inv[None, :]
    cos, sin = jnp.cos(ang), jnp.sin(ang)
    reps = HEAD_W // DH
    cos_t = jnp.tile(jnp.concatenate([cos, cos], axis=1), (1, reps))
    sin_t = jnp.tile(jnp.concatenate([-sin, sin], axis=1), (1, reps))
    return cos_t, sin_t


def _routing_tables(info, counts_f, n_experts, T):
    tm = EXPERT_TILE
    e12 = info[:, 0:2].astype(jnp.int32)
    rank = info[:, 4:6].astype(jnp.int32)
    counts = counts_f[0, :n_experts].astype(jnp.int32)
    padded = (counts + tm - 1) // tm * tm
    ends = jnp.cumsum(padded)
    starts = ends - padded
    onehot = e12[:, :, None] == jnp.arange(n_experts, dtype=jnp.int32)
    pos = jnp.sum(jnp.where(onehot, starts, 0), axis=-1) + rank
    P = T * TOP_K + n_experts * tm
    sub = EXPERT_SUB
    pad = jnp.stack([starts + counts, (counts + sub - 1) // sub * sub - counts])
    blk_start = jnp.arange(P // sub, dtype=jnp.int32) * sub
    holds_rows = jnp.any((blk_start[:, None] >= starts[None, :]) & (blk_start[:, None] < (starts + counts)[None, :]), axis=1)
    slot = jnp.where(holds_rows, -1, jnp.cumsum(~holds_rows) - 1)
    hit = slot[None, :] == jnp.arange(n_experts * tm // sub, dtype=jnp.int32)[:, None]
    empty_blocks = (jnp.sum(jnp.where(hit, jnp.arange(P // sub, dtype=jnp.int32) + 1, 0), axis=1) - 1).astype(jnp.int32)
    n_tiles = P // tm + 1
    tile_start = jnp.arange(n_tiles, dtype=jnp.int32) * tm
    tile_expert = jnp.minimum(jnp.sum(tile_start[:, None] >= ends[None, :], axis=1), n_experts - 1).astype(jnp.int32)
    tile_rows = jnp.clip((starts + counts)[tile_expert] - tile_start, 0, tm).astype(jnp.int32)
    tile_src = jnp.minimum(jnp.arange(n_tiles, dtype=jnp.int32), ends[-1] // tm - 1).astype(jnp.int32)
    return pos, pad, empty_blocks, P, tile_expert, tile_rows, tile_src


def kernel(x, p, norm_mix, norm_ffn, norm_ple, attn_w_qkv, attn_w_o, attn_q_norm, attn_k_norm,
           attn_lambda_q1, attn_lambda_k1, attn_lambda_q2, attn_lambda_k2, attn_subln,
           pool_w, pool_b, pool_scale, ffn_w_gate, ffn_w_up, ffn_w_down,
           moe_router, moe_w_gate, moe_w_up, moe_w_down, ple_w_proj, ple_w_gate):
    B, S, D = x.shape
    T = B * S
    n_experts = moe_router.shape[-1]
    x2 = x.reshape(T, D)
    p3 = p.reshape(p.shape[0], T, p.shape[-1])
    bf = lambda a: a.astype(BF16)
    vec = lambda a: a.reshape(1, -1)

    lam_init = 0.8 - 0.6 * math.exp(-0.3 * 0)
    cos_t, sin_t = _rope_tables(S)
    reps = HEAD_W // DH
    qk_gain = jnp.zeros((8, LANES), F32)
    qk_gain = qk_gain.at[0].set(jnp.tile(attn_q_norm[0], reps) * (ATTN_SCALE * math.log2(math.e)))
    qk_gain = qk_gain.at[1].set(jnp.tile(attn_k_norm[0], reps))
    group = jnp.arange(2 * LANES) // DH
    ones = (group[:, None] == group[None, :]).astype(BF16)
    qkv = _qkv(x2, vec(norm_mix[0]), attn_w_qkv[0], cos_t, sin_t, qk_gain, ones)
    lam_params = jnp.stack([attn_lambda_q1[0], attn_lambda_k1[0], attn_lambda_q2[0], attn_lambda_k2[0]])
    ao = _attention(qkv.reshape(B, S, 3 * D), lam_params, vec(attn_subln[0]), lam_init)
    h = _dense_tail(ao.reshape(T, D), x2, p3, 0, bf(attn_w_o[0]), vec(norm_ffn[0]),
                    bf(ffn_w_gate[0]), bf(ffn_w_up[0]), bf(ffn_w_down[0]),
                    vec(norm_ple[0]), bf(ple_w_gate[0]), bf(ple_w_proj[0]))

    w_router = jnp.zeros((D, LANES), F32).at[:, :n_experts].set(moe_router[0])
    w_router_hi = w_router.astype(BF16)
    w_router_pad = jnp.stack([w_router_hi, (w_router - w_router_hi.astype(F32)).astype(BF16)])
    h, xn, info, counts = _pool_router(h, S, vec(norm_mix[1]), bf(pool_w[0]), vec(pool_b[0]),
                                       vec(pool_scale[0]), vec(norm_ffn[1]), w_router_pad, n_experts)
    pos, pad, empty_blocks, n_sorted, tile_expert, tile_rows, tile_src = _routing_tables(info, counts, n_experts, T)
    xs, inv = _row_scatter(xn, pos, pad, empty_blocks, n_sorted)
    yk = _experts(xs, inv, tile_expert, tile_rows, tile_src, moe_w_gate[0], moe_w_up[0], moe_w_down[0], T)
    out = _combine(h, yk, info, p3, 1, vec(norm_ple[1]), bf(ple_w_gate[1]), bf(ple_w_proj[1]))
    return out.reshape(B, S, D)
```

```python
import functools
import math

import jax
import jax.numpy as jnp
from jax import lax
from jax.experimental import pallas as pl
from jax.experimental.pallas import tpu as pltpu

F32 = jnp.float32
BF16 = jnp.bfloat16

DH = 64
HEAD_W = 2 * DH
ATTN_SCALE = 1.0 / math.sqrt(DH)
ROPE_THETA = 10000.0
POOL_WINDOWS = (2, 4, 8, 16)
POOL_HALO = 16
TOP_K = 2
RMS_EPS = 1e-6
LANES = 128
SUBLANES = 8
MXU_COLS = 256
DENSE_F_CHUNK = 512
VMEM_LIMIT = 56 * 1024 * 1024

ROW_TILE = 512
Q_TILE = 256
ATTN_HEADS_PER_STEP = 2
EXPERT_TILE = 1024
EXPERT_SUB = 512
EXPERT_F_TILE = 512
GATHER_ROWS = 1024


def _rms(x, gain):
    return x * lax.rsqrt(jnp.mean(x * x, axis=-1, keepdims=True) + RMS_EPS) * gain


def _const_spec(shape):
    zeros = (0,) * len(shape)
    return pl.BlockSpec(shape, lambda *_: zeros, pipeline_mode=pl.Buffered(1))


def _params(*sem):
    return pltpu.CompilerParams(dimension_semantics=sem, vmem_limit_bytes=VMEM_LIMIT)


def _qkv_kernel(x_ref, g_ref, w_ref, cos_ref, sin_ref, qkg_ref, ones_ref, o_ref, wb_ref):
    D = x_ref.shape[1]

    @pl.when(pl.program_id(0) == 0)
    def _():
        wb_ref[...] = w_ref[...].astype(BF16)

    xn = _rms(x_ref[...], g_ref[...]).astype(BF16)
    cos = cos_ref[...]
    sin = sin_ref[...]
    lane = lax.broadcasted_iota(jnp.int32, cos.shape, 1)
    first_half = (lane % DH) < (DH // 2)
    ones = ones_ref[...]
    width = ones.shape[0]
    for part in range(3):
        y = jnp.dot(xn, wb_ref[:, part * D:(part + 1) * D], preferred_element_type=F32)
        if part == 2:
            o_ref[:, part * D:(part + 1) * D] = y.astype(BF16)
            continue
        gain = qkg_ref[part:part + 1, :]
        for c in range(0, D, width):
            yc = y[:, c:c + width]
            yy = yc * yc
            hi = yy.astype(BF16)
            lo = (yy - hi.astype(F32)).astype(BF16)
            ss = (jnp.dot(hi, ones, preferred_element_type=F32)
                  + jnp.dot(lo, ones, preferred_element_type=F32))
            yn = yc * lax.rsqrt(ss * (1.0 / DH) + RMS_EPS)
            for s in range(0, width, LANES):
                z = yn[:, s:s + LANES] * gain
                rot = jnp.where(first_half,
                                pltpu.roll(z, LANES - DH // 2, 1),
                                pltpu.roll(z, DH // 2, 1))
                col = part * D + c + s
                o_ref[:, col:col + LANES] = (z * cos + rot * sin).astype(BF16)


def _qkv(x2, gain, w_qkv, cos_t, sin_t, qk_gain, ones):
    T, D = x2.shape
    S = cos_t.shape[0]
    tm = ROW_TILE
    blocks_per_seq = S // tm
    return pl.pallas_call(
        _qkv_kernel,
        grid=(T // tm,),
        in_specs=[
            pl.BlockSpec((tm, D), lambda i: (i, 0)),
            _const_spec((1, D)),
            _const_spec((D, 3 * D)),
            pl.BlockSpec((tm, LANES), lambda i: (i % blocks_per_seq, 0)),
            pl.BlockSpec((tm, LANES), lambda i: (i % blocks_per_seq, 0)),
            _const_spec(qk_gain.shape),
            _const_spec(ones.shape),
        ],
        out_specs=pl.BlockSpec((tm, 3 * D), lambda i: (i, 0)),
        out_shape=jax.ShapeDtypeStruct((T, 3 * D), BF16),
        scratch_shapes=[pltpu.VMEM((D, 3 * D), BF16)],
        compiler_params=_params("arbitrary"),
        name="qkv_rope",
    )(x2, gain, w_qkv, cos_t, sin_t, qk_gain, ones)


def _dot_nt(a, b):
    return lax.dot_general(a, b, (((1,), (1,)), ((), ())), preferred_element_type=F32)


def _attn_kernel(q_ref, k_ref, v_ref, lam_ref, g_ref, o_ref, vext_ref, *, lam_init):
    lp = lam_ref[...]
    lam = (jnp.exp(jnp.sum(lp[0:1] * lp[1:2], axis=-1, keepdims=True))
           - jnp.exp(jnp.sum(lp[2:3] * lp[3:4], axis=-1, keepdims=True)) + lam_init)
    S = q_ref.shape[1]
    tq = Q_TILE
    vlane = lax.broadcasted_iota(jnp.int32, (S, HEAD_W), 1)
    lane = lax.broadcasted_iota(jnp.int32, (tq, HEAD_W), 1)
    row = lax.broadcasted_iota(jnp.int32, (tq, tq), 0)
    col = lax.broadcasted_iota(jnp.int32, (tq, tq), 1)
    future = col > row
    gain = g_ref[...]
    n_heads = q_ref.shape[2] // HEAD_W
    for head in range(n_heads):
        vext_ref[head, :, :HEAD_W] = v_ref[0, :, head * HEAD_W:(head + 1) * HEAD_W]
        vext_ref[head, :, HEAD_W:] = jnp.where(vlane == 0, 1.0, 0.0).astype(BF16)

    def scores(head, i):
        cols = slice(head * HEAD_W, (head + 1) * HEAD_W)
        q = q_ref[0, i * tq:(i + 1) * tq, cols]
        out = []
        for qc in (jnp.where(lane < DH, q, jnp.zeros_like(q)), jnp.where(lane >= DH, q, jnp.zeros_like(q))):
            s_diag = jnp.where(future, -jnp.inf, _dot_nt(qc, k_ref[0, i * tq:(i + 1) * tq, cols]))
            s_off = _dot_nt(qc, k_ref[0, :i * tq, cols]) if i > 0 else None
            out.append((s_off, s_diag))
        return out

    def finish(head, i, maps):
        kv = (i + 1) * tq
        acc = []
        for s_off, s_diag in maps:
            m = jnp.max(s_diag, axis=-1, keepdims=True)
            if s_off is not None:
                m = jnp.maximum(m, jnp.max(s_off, axis=-1, keepdims=True))
                p = jnp.concatenate([jnp.exp2(s_off - m).astype(BF16),
                                     jnp.exp2(s_diag - m).astype(BF16)], axis=1)
            else:
                p = jnp.exp2(s_diag - m).astype(BF16)
            acc.append(jnp.dot(p, vext_ref[head, :kv, :], preferred_element_type=F32))
        o = (acc[0][:, :HEAD_W] * (1.0 / acc[0][:, HEAD_W:HEAD_W + 1])
             - acc[1][:, :HEAD_W] * (lam / acc[1][:, HEAD_W:HEAD_W + 1]))
        o = _rms(o, gain) * (1.0 - lam_init)
        o_ref[0, i * tq:(i + 1) * tq, head * HEAD_W:(head + 1) * HEAD_W] = o.astype(BF16)

    work = [(head, i) for head in range(n_heads) for i in range(S // tq)]
    pending = scores(*work[0])
    for pos, item in enumerate(work):
        upcoming = scores(*work[pos + 1]) if pos + 1 < len(work) else None
        finish(*item, pending)
        pending = upcoming


def _attention(qkv3, lam_params, subln, lam_init):
    B, S, W3 = qkv3.shape
    heads = ATTN_HEADS_PER_STEP
    H = W3 // 3 // (heads * HEAD_W)
    blk = (1, S, heads * HEAD_W)
    return pl.pallas_call(
        functools.partial(_attn_kernel, lam_init=lam_init),
        grid=(B, H),
        in_specs=[
            pl.BlockSpec(blk, lambda b, h: (b, 0, h)),
            pl.BlockSpec(blk, lambda b, h: (b, 0, H + h)),
            pl.BlockSpec(blk, lambda b, h: (b, 0, 2 * H + h)),
            _const_spec(lam_params.shape),
            _const_spec(subln.shape),
        ],
        out_specs=pl.BlockSpec(blk, lambda b, h: (b, 0, h)),
        out_shape=jax.ShapeDtypeStruct((B, S, W3 // 3), BF16),
        scratch_shapes=[pltpu.VMEM((heads, S, 2 * HEAD_W), BF16)],
        compiler_params=_params("parallel", "parallel"),
        name="diff_attn",
    )(qkv3, qkv3, qkv3, lam_params, subln)


def _ple(h, p, g_ple, w_gate, w_proj):
    xn = _rms(h, g_ple).astype(BF16)
    gate = jax.nn.sigmoid(jnp.dot(xn, w_gate, preferred_element_type=F32))
    proj = jnp.dot(p.astype(BF16), w_proj, preferred_element_type=F32)
    return h + gate * proj


def _swiglu_chunks(xn, wg_ref, wu_ref, wd_ref, width):
    F = wg_ref.shape[1]
    chunks = [(lo, min(lo + width, F)) for lo in range(0, F, width)]

    def gate_up(lo, hi):
        return (jnp.dot(xn, wg_ref[:, lo:hi].astype(BF16), preferred_element_type=F32),
                jnp.dot(xn, wu_ref[:, lo:hi].astype(BF16), preferred_element_type=F32))

    out = None
    pending = gate_up(*chunks[0])
    for c, (lo, hi) in enumerate(chunks):
        upcoming = gate_up(*chunks[c + 1]) if c + 1 < len(chunks) else None
        g, u = pending
        act = (g * jax.nn.sigmoid(g) * u).astype(BF16)
        part = jnp.dot(act, wd_ref[lo:hi, :].astype(BF16), preferred_element_type=F32)
        out = part if out is None else out + part
        pending = upcoming
    return out


def _dense_tail_kernel(ao_ref, x_ref, p_ref, wo_ref, gf_ref, wg_ref, wu_ref, wd_ref,
                       gp_ref, wpg_ref, wpp_ref, o_ref, *, f_chunk):
    h = x_ref[...] + jnp.dot(ao_ref[...], wo_ref[...], preferred_element_type=F32)
    xn = _rms(h, gf_ref[...]).astype(BF16)
    acc = h + _swiglu_chunks(xn, wg_ref, wu_ref, wd_ref, f_chunk)
    o_ref[...] = _ple(acc, p_ref[...], gp_ref[...], wpg_ref[...], wpp_ref[...])


def _dense_tail(ao, x2, p3, layer, w_o, g_ffn, w_gate, w_up, w_down, g_ple, w_pg, w_pp):
    T, D = x2.shape
    F = w_gate.shape[1]
    PD = p3.shape[2]
    tm = ROW_TILE
    row = lambda w: pl.BlockSpec((tm, w), lambda i: (i, 0))
    return pl.pallas_call(
        functools.partial(_dense_tail_kernel, f_chunk=DENSE_F_CHUNK),
        grid=(T // tm,),
        in_specs=[row(D), row(D), pl.BlockSpec((None, tm, PD), lambda i: (layer, i, 0)),
                  _const_spec((D, D)), _const_spec((1, D)),
                  _const_spec((D, F)), _const_spec((D, F)), _const_spec((F, D)),
                  _const_spec((1, D)), _const_spec((D, D)), _const_spec((PD, D))],
        out_specs=row(D),
        out_shape=jax.ShapeDtypeStruct((T, D), F32),
        compiler_params=_params("parallel"),
        name="dense_tail",
    )(ao, x2, p3, w_o, g_ffn, w_gate, w_up, w_down, g_ple, w_pg, w_pp)


def _pool_router_kernel(h_ref, halo_ref, gm_ref, pw_ref, pb_ref, ps_ref, gf_ref, wr_ref,
                        h_out_ref, xn_out_ref, info_ref, cnt_ref, xbuf_ref, carry_ref,
                        *, blocks_per_seq, n_experts):
    i = pl.program_id(0)
    tm, D = h_ref.shape
    gw = D // len(POOL_WINDOWS)
    seq_block = i % blocks_per_seq

    @pl.when(i == 0)
    def _():
        carry_ref[...] = jnp.zeros_like(carry_ref)

    h = h_ref[...]
    gm = gm_ref[...]
    xbuf_ref[POOL_HALO:, :] = _rms(h, gm)

    @pl.when(seq_block == 0)
    def _():
        xbuf_ref[:POOL_HALO, :] = jnp.zeros((POOL_HALO, D), F32)

    @pl.when(seq_block != 0)
    def _():
        xbuf_ref[:POOL_HALO, :] = _rms(halo_ref[...], gm)

    t = seq_block * tm + lax.broadcasted_iota(jnp.int32, (tm, 1), 0)
    pieces = []
    for g, w in enumerate(POOL_WINDOWS):
        cs = slice(g * gw, (g + 1) * gw)
        win = xbuf_ref[POOL_HALO:, cs]
        tok = win
        for k in range(1, w):
            win = win + xbuf_ref[POOL_HALO - k:POOL_HALO - k + tm, cs]
        cnt = jnp.minimum(t + 1, w).astype(F32)
        y = (win / cnt - tok).astype(BF16)
        pieces.append(jnp.dot(y, pw_ref[g], preferred_element_type=F32))
    mix = (jnp.concatenate(pieces, axis=1) + pb_ref[...]) * ps_ref[...]
    h2 = h + mix
    h_out_ref[...] = h2
    xn = _rms(h2, gf_ref[...])
    xn_out_ref[...] = xn

    xh = xn.astype(BF16)
    xl = (xn - xh.astype(F32)).astype(BF16)
    logits = (jnp.dot(xh, wr_ref[0], preferred_element_type=F32)
              + jnp.dot(xl, wr_ref[0], preferred_element_type=F32)
              + jnp.dot(xh, wr_ref[1], preferred_element_type=F32))
    lane = lax.broadcasted_iota(jnp.int32, logits.shape, 1)
    lg = jnp.where(lane < n_experts, logits, -jnp.inf)
    m1 = jnp.max(lg, axis=-1, keepdims=True)
    i1 = jnp.min(jnp.where(lg == m1, lane, LANES), axis=-1, keepdims=True)
    lg2 = jnp.where(lane == i1, -jnp.inf, lg)
    m2 = jnp.max(lg2, axis=-1, keepdims=True)
    i2 = jnp.min(jnp.where(lg2 == m2, lane, LANES), axis=-1, keepdims=True)
    e = jnp.exp(m2 - m1)
    w1 = 1.0 / (1.0 + e)
    w2 = e / (1.0 + e)

    sel1 = lane == i1
    sel2 = lane == i2
    onehot = jnp.where(sel1 | sel2, 1.0, 0.0)
    r = lax.broadcasted_iota(jnp.int32, (tm, tm), 0)
    c = lax.broadcasted_iota(jnp.int32, (tm, tm), 1)
    strict_lower = jnp.where(c < r, 1.0, 0.0).astype(BF16)
    before = jnp.dot(strict_lower, onehot.astype(BF16), preferred_element_type=F32) + carry_ref[0:1, :]
    rank1 = jnp.sum(jnp.where(sel1, before, 0.0), axis=-1, keepdims=True)
    rank2 = jnp.sum(jnp.where(sel2, before, 0.0), axis=-1, keepdims=True)
    total = carry_ref[0:1, :] + jnp.sum(onehot, axis=0, keepdims=True)
    carry_ref[...] = jnp.broadcast_to(total, carry_ref.shape)
    cnt_ref[...] = jnp.broadcast_to(total, cnt_ref.shape)

    info = jnp.where(lane == 0, i1.astype(F32), 0.0)
    info = jnp.where(lane == 1, i2.astype(F32), info)
    info = jnp.where(lane == 2, w1, info)
    info = jnp.where(lane == 3, w2, info)
    info = jnp.where(lane == 4, rank1, info)
    info = jnp.where(lane == 5, rank2, info)
    info_ref[...] = info


def _pool_router(h, S, g_mix, pool_w, pool_b, pool_scale, g_ffn, w_router_pad, n_experts):
    T, D = h.shape
    tm = ROW_TILE
    G, gw, _ = pool_w.shape
    halo_per_tile = tm // POOL_HALO
    row = pl.BlockSpec((tm, D), lambda i: (i, 0))
    return pl.pallas_call(
        functools.partial(_pool_router_kernel, blocks_per_seq=S // tm, n_experts=n_experts),
        grid=(T // tm,),
        in_specs=[
            row,
            pl.BlockSpec((POOL_HALO, D), lambda i: (jnp.maximum(i * halo_per_tile - 1, 0), 0)),
            _const_spec((1, D)), _const_spec((G, gw, gw)), _const_spec((1, D)), _const_spec((1, D)),
            _const_spec((1, D)), _const_spec((2, D, LANES)),
        ],
        out_specs=[row, row,
                   pl.BlockSpec((tm, LANES), lambda i: (i, 0)),
                   pl.BlockSpec((8, LANES), lambda i: (0, 0))],
        out_shape=[jax.ShapeDtypeStruct((T, D), F32), jax.ShapeDtypeStruct((T, D), F32),
                   jax.ShapeDtypeStruct((T, LANES), F32), jax.ShapeDtypeStruct((8, LANES), F32)],
        scratch_shapes=[pltpu.VMEM((tm + POOL_HALO, D), F32), pltpu.VMEM((8, LANES), F32)],
        compiler_params=_params("arbitrary"),
        name="pool_router",
    )(h, h, g_mix, pool_w, pool_b, pool_scale, g_ffn, w_router_pad)


def _scatter_kernel(pos_ref, pad_ref, blk_ref, inv_init_ref, x_ref, out_ref, inv_ref, sem):
    n = x_ref.shape[0]
    n_tokens = n * pl.num_programs(0)
    first = pl.program_id(0) * n

    def copy(src_row, n_rows, dst_row):
        return pltpu.make_async_copy(x_ref.at[pl.ds(src_row, n_rows)], out_ref.at[pl.ds(dst_row, n_rows)], sem)

    def issue(r, carry):
        for k in range(TOP_K):
            p = pos_ref[0, k, r]
            copy(r, 1, p).start()
            inv_ref[p] = k * n_tokens + first + r
        return carry

    lax.fori_loop(0, n, issue, 0, unroll=4)

    @pl.when(pl.program_id(0) == pl.num_programs(0) - 1)
    def _():
        for e in range(pad_ref.shape[1]):
            start = pad_ref[0, e]
            count = pad_ref[1, e]

            def fill(r, carry):
                copy(0, 1, start + r).start()
                inv_ref[start + r] = inv_init_ref[start + r]
                return carry

            lax.fori_loop(0, count, fill, 0)

            whole = pl.multiple_of(count // 8 * 8, 8)

            @pl.when(whole > 0)
            def _():
                copy(0, whole, 0).wait()

            def drain(r, carry):
                copy(0, 1, 0).wait()
                return carry

            lax.fori_loop(0, count - whole, drain, 0)

        for k in range(blk_ref.shape[0]):
            blk = blk_ref[k]

            @pl.when(blk >= 0)
            def _():
                c = copy(0, EXPERT_SUB, pl.multiple_of(blk * EXPERT_SUB, EXPERT_SUB))
                c.start()

                def keep(r, carry):
                    inv_ref[blk * EXPERT_SUB + r] = inv_init_ref[blk * EXPERT_SUB + r]
                    return carry

                lax.fori_loop(0, EXPERT_SUB, keep, 0, unroll=8)
                c.wait()

    for k in range(TOP_K):
        copy(0, n, 0).wait()


def _row_scatter(x, pos, pad, empty_blocks, n_out):
    T, D = x.shape
    n = GATHER_ROWS
    assert EXPERT_SUB <= n
    pos_blocks = pos.reshape(T // n, n, TOP_K).transpose(0, 2, 1)
    pad_dst = TOP_K * T + jnp.arange(n_out, dtype=jnp.int32) % EXPERT_TILE
    return pl.pallas_call(
        _scatter_kernel,
        grid=(T // n,),
        in_specs=[pl.BlockSpec((1, TOP_K, n), lambda i: (i, 0, 0), memory_space=pltpu.SMEM),
                  pl.BlockSpec(memory_space=pltpu.SMEM),
                  pl.BlockSpec(memory_space=pltpu.SMEM),
                  pl.BlockSpec(memory_space=pltpu.SMEM),
                  pl.BlockSpec((n, D), lambda i: (i, 0))],
        out_specs=[pl.BlockSpec(memory_space=pl.ANY), pl.BlockSpec(memory_space=pltpu.SMEM)],
        out_shape=[jax.ShapeDtypeStruct((n_out, D), x.dtype), jax.ShapeDtypeStruct((n_out,), jnp.int32)],
        scratch_shapes=[pltpu.SemaphoreType.DMA],
        compiler_params=_params("arbitrary"),
        name="row_scatter",
    )(pos_blocks, pad, empty_blocks, pad_dst, x)


def _expert_kernel(te_ref, tr_ref, tx_ref, inv_ref, x_ref, wg_ref, wu_ref, wd_ref, yk_ref, yo_ref, sem,
                   *, n_f, dump_base):
    i = pl.program_id(0)
    j = pl.program_id(1)
    tm, D = x_ref.shape
    sub = EXPERT_SUB
    chunk = tm // n_f // SUBLANES * SUBLANES
    rest = tm - n_f * chunk
    slot = i % 2
    prev = 1 - slot
    n_live = (tr_ref[i] + sub - 1) // sub

    def row_copy(buf, row, dst):
        return pltpu.make_async_copy(yo_ref.at[buf, pl.ds(row, 1)], yk_ref.at[pl.ds(dst, 1)], sem.at[buf])

    def wait_rows(buf):
        pltpu.make_async_copy(yo_ref.at[buf], yk_ref.at[pl.ds(0, tm)], sem.at[buf]).wait()

    def send_prev_rows(first_row, count):
        first_row = pl.multiple_of(first_row, SUBLANES)
        table = jnp.maximum(i - 1, 0) * tm + first_row
        for q in range(count):
            dst = jnp.where(i == 0, dump_base + first_row + q, inv_ref[table + q])
            row_copy(prev, first_row + q, dst).start()

    @pl.when((i == 0) & (j == 0))
    def _():
        yo_ref[...] = jnp.zeros(yo_ref.shape, F32)

    def sends_rows(t):
        return (t == 0) | (tr_ref[jnp.maximum(t - 1, 0)] > 0)

    @pl.when((i > 0) & (j == 0) & sends_rows(i - 1))
    def _():
        wait_rows(slot)

    for n in range(1, tm // sub + 1):
        m = n * sub

        @pl.when(n_live == n)
        def _():
            send_prev_rows(j * chunk, chunk)
            x = x_ref[:m, :].astype(BF16)
            part = _swiglu_chunks(x, wg_ref.at[0], wu_ref.at[0], wd_ref.at[0], MXU_COLS)
            yo_ref[slot, :m, :] = jnp.where(j > 0, yo_ref[slot, :m, :], 0.0) + part

            @pl.when(j == 0)
            def _():
                send_prev_rows(n_f * chunk, rest)
                if m < tm:
                    yo_ref[slot, m:, :] = jnp.zeros((tm - m, D), F32)

    @pl.when((n_live == 0) & sends_rows(i))
    def _():
        send_prev_rows(j * chunk, chunk)

        @pl.when(j == 0)
        def _():
            send_prev_rows(n_f * chunk, rest)

    @pl.when((i == pl.num_programs(0) - 1) & (j == n_f - 1) & sends_rows(i))
    def _():
        wait_rows(prev)


def _experts(xs, inv, tile_expert, tile_rows, tile_src, w_gate, w_up, w_down, n_tokens):
    P = xs.shape[0]
    E, D, F = w_gate.shape
    tm, tf = EXPERT_TILE, EXPERT_F_TILE
    n_f = F // tf
    dump_base = TOP_K * n_tokens
    dump_rows = tm

    def row_map(i, j, te, tr, tx, inv):
        return (tx[i], 0)

    def col_w(i, j, te, tr, tx, inv):
        return (te[i], 0, jnp.where(tr[i] > 0, j, n_f - 1))

    def row_w(i, j, te, tr, tx, inv):
        return (te[i], jnp.where(tr[i] > 0, j, n_f - 1), 0)

    grid_spec = pltpu.PrefetchScalarGridSpec(
        num_scalar_prefetch=4,
        grid=(P // tm + 1, n_f),
        in_specs=[pl.BlockSpec((tm, D), row_map),
                  pl.BlockSpec((1, D, tf), col_w),
                  pl.BlockSpec((1, D, tf), col_w),
                  pl.BlockSpec((1, tf, D), row_w)],
        out_specs=pl.BlockSpec(memory_space=pl.ANY),
        scratch_shapes=[pltpu.VMEM((2, tm, D), F32), pltpu.SemaphoreType.DMA((2,))],
    )
    return pl.pallas_call(
        functools.partial(_expert_kernel, n_f=n_f, dump_base=dump_base),
        grid_spec=grid_spec,
        out_shape=jax.ShapeDtypeStruct((dump_base + dump_rows, D), F32),
        compiler_params=_params("arbitrary", "arbitrary"),
        name="expert_swiglu",
    )(tile_expert, tile_rows, tile_src, inv, xs, w_gate, w_up, w_down)


def _combine_kernel(h_ref, y1_ref, y2_ref, info_ref, p_ref, gp_ref, wpg_ref, wpp_ref, o_ref):
    info = info_ref[...]
    h = h_ref[...] + info[:, 2:3] * y1_ref[...] + info[:, 3:4] * y2_ref[...]
    o_ref[...] = _ple(h, p_ref[...], gp_ref[...], wpg_ref[...], wpp_ref[...])


def _combine(h, yk, info, p3, layer, g_ple, w_pg, w_pp):
    T, D = h.shape
    PD = p3.shape[2]
    tm = ROW_TILE
    nb = T // tm
    row = lambda w: pl.BlockSpec((tm, w), lambda i: (i, 0))
    return pl.pallas_call(
        _combine_kernel,
        grid=(nb,),
        in_specs=[row(D), row(D), pl.BlockSpec((tm, D), lambda i: (nb + i, 0)), row(LANES),
                  pl.BlockSpec((None, tm, PD), lambda i: (layer, i, 0)),
                  _const_spec((1, D)), _const_spec((D, D)), _const_spec((PD, D))],
        out_specs=row(D),
        out_shape=jax.ShapeDtypeStruct((T, D), F32),
        compiler_params=_params("parallel"),
        name="combine_ple",
    )(h, yk, yk, info, p3, g_ple, w_pg, w_pp)


def _rope_tables(S):
    inv = 1.0 / (ROPE_THETA ** (jnp.arange(0, DH, 2, dtype=F32) / DH))
    ang = jnp.arange(S, dtype=F32)[:, None] * inv[None, :]
    cos, sin = jnp.cos(ang), jnp.sin(ang)
    reps = HEAD_W // DH
    cos_t = jnp.tile(jnp.concatenate([cos, cos], axis=1), (1, reps))
    sin_t = jnp.tile(jnp.concatenate([-sin, sin], axis=1), (1, reps))
    return cos_t, sin_t


def _routing_tables(info, counts_f, n_experts, T):
    tm = EXPERT_TILE
    e12 = info[:, 0:2].astype(jnp.int32)
    rank = info[:, 4:6].astype(jnp.int32)
    counts = counts_f[0, :n_experts].astype(jnp.int32)
    padded = (counts + tm - 1) // tm * tm
    ends = jnp.cumsum(padded)
    starts = ends - padded
    onehot = e12[:, :, None] == jnp.arange(n_experts, dtype=jnp.int32)
    pos = jnp.sum(jnp.where(onehot, starts, 0), axis=-1) + rank
    P = T * TOP_K + n_experts * tm
    sub = EXPERT_SUB
    pad = jnp.stack([starts + counts, (counts + sub - 1) // sub * sub - counts])
    blk_start = jnp.arange(P // sub, dtype=jnp.int32) * sub
    holds_rows = jnp.any((blk_start[:, None] >= starts[None, :]) & (blk_start[:, None] < (starts + counts)[None, :]), axis=1)
    slot = jnp.where(holds_rows, -1, jnp.cumsum(~holds_rows) - 1)
    hit = slot[None, :] == jnp.arange(n_experts * tm // sub, dtype=jnp.int32)[:, None]
    empty_blocks = (jnp.sum(jnp.where(hit, jnp.arange(P // sub, dtype=jnp.int32) + 1, 0), axis=1) - 1).astype(jnp.int32)
    n_tiles = P // tm + 1
    tile_start = jnp.arange(n_tiles, dtype=jnp.int32) * tm
    tile_expert = jnp.minimum(jnp.sum(tile_start[:, None] >= ends[None, :], axis=1), n_experts - 1).astype(jnp.int32)
    tile_rows = jnp.clip((starts + counts)[tile_expert] - tile_start, 0, tm).astype(jnp.int32)
    tile_src = jnp.minimum(jnp.arange(n_tiles, dtype=jnp.int32), ends[-1] // tm - 1).astype(jnp.int32)
    return pos, pad, empty_blocks, P, tile_expert, tile_rows, tile_src


def kernel(x, p, norm_mix, norm_ffn, norm_ple, attn_w_qkv, attn_w_o, attn_q_norm, attn_k_norm,
           attn_lambda_q1, attn_lambda_k1, attn_lambda_q2, attn_lambda_k2, attn_subln,
           pool_w, pool_b, pool_scale, ffn_w_gate, ffn_w_up, ffn_w_down,
           moe_router, moe_w_gate, moe_w_up, moe_w_down, ple_w_proj, ple_w_gate):
    B, S, D = x.shape
    T = B * S
    n_experts = moe_router.shape[-1]
    x2 = x.reshape(T, D)
    p3 = p.reshape(p.shape[0], T, p.shape[-1])
    bf = lambda a: a.astype(BF16)
    vec = lambda a: a.reshape(1, -1)

    lam_init = 0.8 - 0.6 * math.exp(-0.3 * 0)
    cos_t, sin_t = _rope_tables(S)
    reps = HEAD_W // DH
    qk_gain = jnp.zeros((8, LANES), F32)
    qk_gain = qk_gain.at[0].set(jnp.tile(attn_q_norm[0], reps) * (ATTN_SCALE * math.log2(math.e)))
    qk_gain = qk_gain.at[1].set(jnp.tile(attn_k_norm[0], reps))
    group = jnp.arange(2 * LANES) // DH
    ones = (group[:, None] == group[None, :]).astype(BF16)
    qkv = _qkv(x2, vec(norm_mix[0]), attn_w_qkv[0], cos_t, sin_t, qk_gain, ones)
    lam_params = jnp.stack([attn_lambda_q1[0], attn_lambda_k1[0], attn_lambda_q2[0], attn_lambda_k2[0]])
    ao = _attention(qkv.reshape(B, S, 3 * D), lam_params, vec(attn_subln[0]), lam_init)
    h = _dense_tail(ao.reshape(T, D), x2, p3, 0, bf(attn_w_o[0]), vec(norm_ffn[0]),
                    bf(ffn_w_gate[0]), bf(ffn_w_up[0]), bf(ffn_w_down[0]),
                    vec(norm_ple[0]), bf(ple_w_gate[0]), bf(ple_w_proj[0]))

    w_router = jnp.zeros((D, LANES), F32).at[:, :n_experts].set(moe_router[0])
    w_router_hi = w_router.astype(BF16)
    w_router_pad = jnp.stack([w_router_hi, (w_router - w_router_hi.astype(F32)).astype(BF16)])
    h, xn, info, counts = _pool_router(h, S, vec(norm_mix[1]), bf(pool_w[0]), vec(pool_b[0]),
                                       vec(pool_scale[0]), vec(norm_ffn[1]), w_router_pad, n_experts)
    pos, pad, empty_blocks, n_sorted, tile_expert, tile_rows, tile_src = _routing_tables(info, counts, n_experts, T)
    xs, inv = _row_scatter(xn, pos, pad, empty_blocks, n_sorted)
    yk = _experts(xs, inv, tile_expert, tile_rows, tile_src, moe_w_gate[0], moe_w_up[0], moe_w_down[0], T)
    out = _combine(h, yk, info, p3, 1, vec(norm_ple[1]), bf(ple_w_gate[1]), bf(ple_w_proj[1]))
    return out.reshape(B, S, D)
```

```python
import functools
import math

import jax
import jax.numpy as jnp
from jax import lax
from jax.experimental import pallas as pl
from jax.experimental.pallas import tpu as pltpu

F32 = jnp.float32
BF16 = jnp.bfloat16

DH = 64
HEAD_W = 2 * DH
ATTN_SCALE = 1.0 / math.sqrt(DH)
ROPE_THETA = 10000.0
POOL_WINDOWS = (2, 4, 8, 16)
POOL_HALO = 32
TOP_K = 2
RMS_EPS = 1e-6
LANES = 128
SUBLANES = 8
MXU_COLS = 256
DENSE_F_CHUNK = 512
VMEM_LIMIT = 56 * 1024 * 1024

ROW_TILE = 512
COMBINE_TILE = 1024
Q_TILE = 256
ATTN_HEADS_PER_STEP = 2
EXPERT_TILE = 1024
EXPERT_SUB = 256
EXPERT_F_TILE = 512
GATHER_ROWS = 1024


def _rms(x, gain):
    return x * lax.rsqrt(jnp.mean(x * x, axis=-1, keepdims=True) + RMS_EPS) * gain


def _const_spec(shape):
    zeros = (0,) * len(shape)
    return pl.BlockSpec(shape, lambda *_: zeros, pipeline_mode=pl.Buffered(1))


def _params(*sem):
    return pltpu.CompilerParams(dimension_semantics=sem, vmem_limit_bytes=VMEM_LIMIT)


def _qkv_kernel(x_ref, g_ref, w_ref, cos_ref, sin_ref, qkg_ref, ones_ref, o_ref, wb_ref):
    D = x_ref.shape[1]

    @pl.when(pl.program_id(0) == 0)
    def _():
        wb_ref[...] = w_ref[...].astype(BF16)

    xn = _rms(x_ref[...], g_ref[...]).astype(BF16)
    cos = cos_ref[...]
    sin = sin_ref[...]
    lane = lax.broadcasted_iota(jnp.int32, cos.shape, 1)
    first_half = (lane % DH) < (DH // 2)
    ones = ones_ref[...]
    width = ones.shape[0]
    def project(col):
        return jnp.dot(xn, wb_ref[:, col:col + width], preferred_element_type=F32)

    def finish(col, yc):
        part = col // D
        if part == 2:
            o_ref[:, col:col + width] = yc.astype(BF16)
            return
        gain = qkg_ref[part:part + 1, :]
        yy = yc * yc
        hi = yy.astype(BF16)
        lo = (yy - hi.astype(F32)).astype(BF16)
        ss = (jnp.dot(hi, ones, preferred_element_type=F32)
              + jnp.dot(lo, ones, preferred_element_type=F32))
        yn = yc * lax.rsqrt(ss * (1.0 / DH) + RMS_EPS)
        for s in range(0, width, LANES):
            z = yn[:, s:s + LANES] * gain
            rot = jnp.where(first_half,
                            pltpu.roll(z, LANES - DH // 2, 1),
                            pltpu.roll(z, DH // 2, 1))
            o_ref[:, col + s:col + s + LANES] = (z * cos + rot * sin).astype(BF16)

    cols = list(range(0, 3 * D, width))
    pending = project(cols[0])
    for n, col in enumerate(cols):
        upcoming = project(cols[n + 1]) if n + 1 < len(cols) else None
        finish(col, pending)
        pending = upcoming


def _qkv(x2, gain, w_qkv, cos_t, sin_t, qk_gain, ones):
    T, D = x2.shape
    S = cos_t.shape[0]
    tm = ROW_TILE
    blocks_per_seq = S // tm
    return pl.pallas_call(
        _qkv_kernel,
        grid=(T // tm,),
        in_specs=[
            pl.BlockSpec((tm, D), lambda i: (i, 0)),
            _const_spec((1, D)),
            _const_spec((D, 3 * D)),
            pl.BlockSpec((tm, LANES), lambda i: (i % blocks_per_seq, 0)),
            pl.BlockSpec((tm, LANES), lambda i: (i % blocks_per_seq, 0)),
            _const_spec(qk_gain.shape),
            _const_spec(ones.shape),
        ],
        out_specs=pl.BlockSpec((tm, 3 * D), lambda i: (i, 0)),
        out_shape=jax.ShapeDtypeStruct((T, 3 * D), BF16),
        scratch_shapes=[pltpu.VMEM((D, 3 * D), BF16)],
        compiler_params=_params("arbitrary"),
        name="qkv_rope",
    )(x2, gain, w_qkv, cos_t, sin_t, qk_gain, ones)


def _dot_nt(a, b):
    return lax.dot_general(a, b, (((1,), (1,)), ((), ())), preferred_element_type=F32)


def _attn_kernel(q_ref, k_ref, v_ref, lam_ref, g_ref, o_ref, vext_ref, *, lam_init):
    lp = lam_ref[...]
    lam = (jnp.exp(jnp.sum(lp[0:1] * lp[1:2], axis=-1, keepdims=True))
           - jnp.exp(jnp.sum(lp[2:3] * lp[3:4], axis=-1, keepdims=True)) + lam_init)
    S = q_ref.shape[1]
    tq = Q_TILE
    vlane = lax.broadcasted_iota(jnp.int32, (S, HEAD_W), 1)
    lane = lax.broadcasted_iota(jnp.int32, (tq, HEAD_W), 1)
    row = lax.broadcasted_iota(jnp.int32, (tq, tq), 0)
    col = lax.broadcasted_iota(jnp.int32, (tq, tq), 1)
    future = col > row
    gain = g_ref[...]
    n_heads = q_ref.shape[2] // HEAD_W
    for head in range(n_heads):
        vext_ref[head, :, :HEAD_W] = v_ref[0, :, head * HEAD_W:(head + 1) * HEAD_W]
        vext_ref[head, :, HEAD_W:] = jnp.where(vlane == 0, 1.0, 0.0).astype(BF16)

    def scores(head, i):
        cols = slice(head * HEAD_W, (head + 1) * HEAD_W)
        q = q_ref[0, i * tq:(i + 1) * tq, cols]
        out = []
        for qc in (jnp.where(lane < DH, q, jnp.zeros_like(q)), jnp.where(lane >= DH, q, jnp.zeros_like(q))):
            s_diag = jnp.where(future, -jnp.inf, _dot_nt(qc, k_ref[0, i * tq:(i + 1) * tq, cols]))
            s_off = _dot_nt(qc, k_ref[0, :i * tq, cols]) if i > 0 else None
            out.append((s_off, s_diag))
        return out

    def finish(head, i, maps):
        kv = (i + 1) * tq
        acc = []
        for s_off, s_diag in maps:
            m = jnp.max(s_diag, axis=-1, keepdims=True)
            if s_off is not None:
                m = jnp.maximum(m, jnp.max(s_off, axis=-1, keepdims=True))
                p = jnp.concatenate([jnp.exp2(s_off - m).astype(BF16),
                                     jnp.exp2(s_diag - m).astype(BF16)], axis=1)
            else:
                p = jnp.exp2(s_diag - m).astype(BF16)
            acc.append(jnp.dot(p, vext_ref[head, :kv, :], preferred_element_type=F32))
        o = (acc[0][:, :HEAD_W] * (1.0 / acc[0][:, HEAD_W:HEAD_W + 1])
             - acc[1][:, :HEAD_W] * (lam / acc[1][:, HEAD_W:HEAD_W + 1]))
        o = _rms(o, gain) * (1.0 - lam_init)
        o_ref[0, i * tq:(i + 1) * tq, head * HEAD_W:(head + 1) * HEAD_W] = o.astype(BF16)

    work = [(head, i) for head in range(n_heads) for i in range(S // tq)]
    pending = scores(*work[0])
    for pos, item in enumerate(work):
        upcoming = scores(*work[pos + 1]) if pos + 1 < len(work) else None
        finish(*item, pending)
        pending = upcoming


def _attention(qkv3, lam_params, subln, lam_init):
    B, S, W3 = qkv3.shape
    heads = ATTN_HEADS_PER_STEP
    H = W3 // 3 // (heads * HEAD_W)
    blk = (1, S, heads * HEAD_W)
    return pl.pallas_call(
        functools.partial(_attn_kernel, lam_init=lam_init),
        grid=(B, H),
        in_specs=[
            pl.BlockSpec(blk, lambda b, h: (b, 0, h)),
            pl.BlockSpec(blk, lambda b, h: (b, 0, H + h)),
            pl.BlockSpec(blk, lambda b, h: (b, 0, 2 * H + h)),
            _const_spec(lam_params.shape),
            _const_spec(subln.shape),
        ],
        out_specs=pl.BlockSpec(blk, lambda b, h: (b, 0, h)),
        out_shape=jax.ShapeDtypeStruct((B, S, W3 // 3), BF16),
        scratch_shapes=[pltpu.VMEM((heads, S, 2 * HEAD_W), BF16)],
        compiler_params=_params("parallel", "parallel"),
        name="diff_attn",
    )(qkv3, qkv3, qkv3, lam_params, subln)


def _ple(h, p, g_ple, w_gate, w_proj):
    xn = _rms(h, g_ple).astype(BF16)
    gate = jax.nn.sigmoid(jnp.dot(xn, w_gate, preferred_element_type=F32))
    proj = jnp.dot(p.astype(BF16), w_proj, preferred_element_type=F32)
    return h + gate * proj


def _swiglu_chunks(xn, wg_ref, wu_ref, wd_ref, width):
    F = wg_ref.shape[1]
    chunks = [(lo, min(lo + width, F)) for lo in range(0, F, width)]

    def gate_up(lo, hi):
        return (jnp.dot(xn, wg_ref[:, lo:hi].astype(BF16), preferred_element_type=F32),
                jnp.dot(xn, wu_ref[:, lo:hi].astype(BF16), preferred_element_type=F32))

    out = None
    pending = gate_up(*chunks[0])
    for c, (lo, hi) in enumerate(chunks):
        upcoming = gate_up(*chunks[c + 1]) if c + 1 < len(chunks) else None
        g, u = pending
        act = (g * jax.nn.sigmoid(g) * u).astype(BF16)
        part = jnp.dot(act, wd_ref[lo:hi, :].astype(BF16), preferred_element_type=F32)
        out = part if out is None else out + part
        pending = upcoming
    return out


def _dense_tail_kernel(ao_ref, x_ref, p_ref, wo_ref, gf_ref, wg_ref, wu_ref, wd_ref,
                       gp_ref, wpg_ref, wpp_ref, o_ref, *, f_chunk):
    h = x_ref[...] + jnp.dot(ao_ref[...], wo_ref[...], preferred_element_type=F32)
    xn = _rms(h, gf_ref[...]).astype(BF16)
    acc = h + _swiglu_chunks(xn, wg_ref, wu_ref, wd_ref, f_chunk)
    o_ref[...] = _ple(acc, p_ref[...], gp_ref[...], wpg_ref[...], wpp_ref[...])


def _dense_tail(ao, x2, p3, layer, w_o, g_ffn, w_gate, w_up, w_down, g_ple, w_pg, w_pp):
    T, D = x2.shape
    F = w_gate.shape[1]
    PD = p3.shape[2]
    tm = ROW_TILE
    row = lambda w: pl.BlockSpec((tm, w), lambda i: (i, 0))
    return pl.pallas_call(
        functools.partial(_dense_tail_kernel, f_chunk=DENSE_F_CHUNK),
        grid=(T // tm,),
        in_specs=[row(D), row(D), pl.BlockSpec((None, tm, PD), lambda i: (layer, i, 0)),
                  _const_spec((D, D)), _const_spec((1, D)),
                  _const_spec((D, F)), _const_spec((D, F)), _const_spec((F, D)),
                  _const_spec((1, D)), _const_spec((D, D)), _const_spec((PD, D))],
        out_specs=row(D),
        out_shape=jax.ShapeDtypeStruct((T, D), F32),
        compiler_params=_params("parallel"),
        name="dense_tail",
    )(ao, x2, p3, w_o, g_ffn, w_gate, w_up, w_down, g_ple, w_pg, w_pp)


def _pool_router_kernel(h_ref, halo_ref, gm_ref, pw_ref, pb_ref, ps_ref, gf_ref, wr_ref,
                        h_out_ref, xn_out_ref, info_ref, cnt_ref, xbuf_ref, carry_ref, *stage_refs,
                        blocks_per_seq, n_experts):
    i = pl.program_id(0)
    tm, D = h_ref.shape
    gw = D // len(POOL_WINDOWS)
    seq_block = i % blocks_per_seq

    @pl.when(i == 0)
    def _():
        carry_ref[...] = jnp.zeros_like(carry_ref)

    h = h_ref[...]
    gm = gm_ref[...]
    xbuf_ref[POOL_HALO:, :] = _rms(h, gm)

    @pl.when(seq_block == 0)
    def _():
        xbuf_ref[:POOL_HALO, :] = jnp.zeros((POOL_HALO, D), F32)

    @pl.when(seq_block != 0)
    def _():
        xbuf_ref[:POOL_HALO, :] = _rms(halo_ref[...], gm)

    n_rows = tm + POOL_HALO
    src, lo = xbuf_ref, 0
    windows = []
    for g, w in enumerate(POOL_WINDOWS):
        shift = w // 2
        assert w == 2 << g
        lo = -(-(lo + shift) // SUBLANES) * SUBLANES
        assert lo <= POOL_HALO
        val = src[lo:n_rows, g * gw:] + src[lo - shift:n_rows - shift, g * gw:]
        if g + 1 < len(POOL_WINDOWS):
            dst = stage_refs[g % 2]
            dst[lo:n_rows, g * gw:] = val
            windows.append(dst[POOL_HALO:n_rows, g * gw:(g + 1) * gw])
            src = dst
        else:
            windows.append(val[POOL_HALO - lo:, :])

    t = seq_block * tm + lax.broadcasted_iota(jnp.int32, (tm, 1), 0)
    pieces = []
    for g, w in enumerate(POOL_WINDOWS):
        tok = xbuf_ref[POOL_HALO:, g * gw:(g + 1) * gw]
        cnt = jnp.minimum(t + 1, w).astype(F32)
        y = (windows[g] / cnt - tok).astype(BF16)
        pieces.append(jnp.dot(y, pw_ref[g], preferred_element_type=F32))
    mix = (jnp.concatenate(pieces, axis=1) + pb_ref[...]) * ps_ref[...]
    h2 = h + mix
    h_out_ref[...] = h2
    xn = _rms(h2, gf_ref[...])
    xn_out_ref[...] = xn

    xh = xn.astype(BF16)
    xl = (xn - xh.astype(F32)).astype(BF16)
    logits = (jnp.dot(xh, wr_ref[0], preferred_element_type=F32)
              + jnp.dot(xl, wr_ref[0], preferred_element_type=F32)
              + jnp.dot(xh, wr_ref[1], preferred_element_type=F32))
    lane = lax.broadcasted_iota(jnp.int32, logits.shape, 1)
    lg = jnp.where(lane < n_experts, logits, -jnp.inf)
    m1 = jnp.max(lg, axis=-1, keepdims=True)
    i1 = jnp.min(jnp.where(lg == m1, lane, LANES), axis=-1, keepdims=True)
    lg2 = jnp.where(lane == i1, -jnp.inf, lg)
    m2 = jnp.max(lg2, axis=-1, keepdims=True)
    i2 = jnp.min(jnp.where(lg2 == m2, lane, LANES), axis=-1, keepdims=True)
    e = jnp.exp(m2 - m1)
    w1 = 1.0 / (1.0 + e)
    w2 = e / (1.0 + e)

    sel1 = lane == i1
    sel2 = lane == i2
    onehot = jnp.where(sel1 | sel2, 1.0, 0.0)
    r = lax.broadcasted_iota(jnp.int32, (tm, tm), 0)
    c = lax.broadcasted_iota(jnp.int32, (tm, tm), 1)
    strict_lower = jnp.where(c < r, 1.0, 0.0).astype(BF16)
    before = jnp.dot(strict_lower, onehot.astype(BF16), preferred_element_type=F32) + carry_ref[0:1, :]
    rank1 = jnp.sum(jnp.where(sel1, before, 0.0), axis=-1, keepdims=True)
    rank2 = jnp.sum(jnp.where(sel2, before, 0.0), axis=-1, keepdims=True)
    total = carry_ref[0:1, :] + jnp.sum(onehot, axis=0, keepdims=True)
    carry_ref[...] = jnp.broadcast_to(total, carry_ref.shape)
    cnt_ref[...] = jnp.broadcast_to(total, cnt_ref.shape)

    info = jnp.where(lane == 0, i1.astype(F32), 0.0)
    info = jnp.where(lane == 1, i2.astype(F32), info)
    info = jnp.where(lane == 2, w1, info)
    info = jnp.where(lane == 3, w2, info)
    info = jnp.where(lane == 4, rank1, info)
    info = jnp.where(lane == 5, rank2, info)
    info_ref[...] = info


def _pool_router(h, S, g_mix, pool_w, pool_b, pool_scale, g_ffn, w_router_pad, n_experts):
    T, D = h.shape
    tm = ROW_TILE
    G, gw, _ = pool_w.shape
    halo_per_tile = tm // POOL_HALO
    row = pl.BlockSpec((tm, D), lambda i: (i, 0))
    return pl.pallas_call(
        functools.partial(_pool_router_kernel, blocks_per_seq=S // tm, n_experts=n_experts),
        grid=(T // tm,),
        in_specs=[
            row,
            pl.BlockSpec((POOL_HALO, D), lambda i: (jnp.maximum(i * halo_per_tile - 1, 0), 0)),
            _const_spec((1, D)), _const_spec((G, gw, gw)), _const_spec((1, D)), _const_spec((1, D)),
            _const_spec((1, D)), _const_spec((2, D, LANES)),
        ],
        out_specs=[row, row,
                   pl.BlockSpec((tm, LANES), lambda i: (i, 0)),
                   pl.BlockSpec((8, LANES), lambda i: (0, 0))],
        out_shape=[jax.ShapeDtypeStruct((T, D), F32), jax.ShapeDtypeStruct((T, D), F32),
                   jax.ShapeDtypeStruct((T, LANES), F32), jax.ShapeDtypeStruct((8, LANES), F32)],
        scratch_shapes=[pltpu.VMEM((tm + POOL_HALO, D), F32), pltpu.VMEM((8, LANES), F32),
                        pltpu.VMEM((tm + POOL_HALO, D), F32), pltpu.VMEM((tm + POOL_HALO, D), F32)],
        compiler_params=_params("arbitrary"),
        name="pool_router",
    )(h, h, g_mix, pool_w, pool_b, pool_scale, g_ffn, w_router_pad)


def _scatter_kernel(pos_ref, pad_ref, blk_ref, inv_init_ref, x_ref, out_ref, inv_ref, sem):
    n = x_ref.shape[0]
    n_tokens = n * pl.num_programs(0)
    first = pl.program_id(0) * n

    def copy(src_row, n_rows, dst_row):
        return pltpu.make_async_copy(x_ref.at[pl.ds(src_row, n_rows)], out_ref.at[pl.ds(dst_row, n_rows)], sem)

    def issue(group, carry):
        r0 = pl.multiple_of(group * SUBLANES, SUBLANES)
        for q in range(SUBLANES):
            for k in range(TOP_K):
                p = pos_ref[0, k, r0 + q]
                copy(r0 + q, 1, p).start()
                inv_ref[p] = k * n_tokens + first + r0 + q
        return carry

    lax.fori_loop(0, n // SUBLANES, issue, 0)

    @pl.when(pl.program_id(0) == pl.num_programs(0) - 1)
    def _():
        for e in range(pad_ref.shape[1]):
            start = pad_ref[0, e]
            count = pad_ref[1, e]

            def fill(r, carry):
                copy(0, 1, start + r).start()
                inv_ref[start + r] = inv_init_ref[start + r]
                return carry

            lax.fori_loop(0, count, fill, 0)

            whole = pl.multiple_of(count // 8 * 8, 8)

            @pl.when(whole > 0)
            def _():
                copy(0, whole, 0).wait()

            def drain(r, carry):
                copy(0, 1, 0).wait()
                return carry

            lax.fori_loop(0, count - whole, drain, 0)

        for k in range(blk_ref.shape[0]):
            blk = blk_ref[k]

            @pl.when(blk >= 0)
            def _():
                c = copy(0, EXPERT_SUB, pl.multiple_of(blk * EXPERT_SUB, EXPERT_SUB))
                c.start()

                def keep(r, carry):
                    inv_ref[blk * EXPERT_SUB + r] = inv_init_ref[blk * EXPERT_SUB + r]
                    return carry

                lax.fori_loop(0, EXPERT_SUB, keep, 0, unroll=8)
                c.wait()

    for k in range(TOP_K):
        copy(0, n, 0).wait()


def _row_scatter(x, pos, pad, empty_blocks, n_out):
    T, D = x.shape
    n = GATHER_ROWS
    assert EXPERT_SUB <= n
    pos_blocks = pos.reshape(T // n, n, TOP_K).transpose(0, 2, 1)
    pad_dst = TOP_K * T + jnp.arange(n_out, dtype=jnp.int32) % EXPERT_TILE
    return pl.pallas_call(
        _scatter_kernel,
        grid=(T // n,),
        in_specs=[pl.BlockSpec((1, TOP_K, n), lambda i: (i, 0, 0), memory_space=pltpu.SMEM),
                  pl.BlockSpec(memory_space=pltpu.SMEM),
                  pl.BlockSpec(memory_space=pltpu.SMEM),
                  pl.BlockSpec(memory_space=pltpu.SMEM),
                  pl.BlockSpec((n, D), lambda i: (i, 0))],
        out_specs=[pl.BlockSpec(memory_space=pl.ANY), pl.BlockSpec(memory_space=pltpu.SMEM)],
        out_shape=[jax.ShapeDtypeStruct((n_out, D), x.dtype), jax.ShapeDtypeStruct((n_out,), jnp.int32)],
        scratch_shapes=[pltpu.SemaphoreType.DMA],
        compiler_params=_params("arbitrary"),
        name="row_scatter",
    )(pos_blocks, pad, empty_blocks, pad_dst, x)


def _expert_kernel(te_ref, tr_ref, tx_ref, inv_ref, x_ref, wg_ref, wu_ref, wd_ref, yk_ref, yo_ref, sem,
                   *, n_f, dump_base):
    i = pl.program_id(0)
    j = pl.program_id(1)
    tm, D = x_ref.shape
    sub = EXPERT_SUB
    chunk = tm // n_f // SUBLANES * SUBLANES
    rest = tm - n_f * chunk
    slot = i % 2
    prev = 1 - slot
    n_live = (tr_ref[i] + sub - 1) // sub

    def row_copy(buf, row, dst):
        return pltpu.make_async_copy(yo_ref.at[buf, pl.ds(row, 1)], yk_ref.at[pl.ds(dst, 1)], sem.at[buf])

    def wait_rows(buf):
        pltpu.make_async_copy(yo_ref.at[buf], yk_ref.at[pl.ds(0, tm)], sem.at[buf]).wait()

    def send_prev_rows(first_row, count):
        first_row = pl.multiple_of(first_row, SUBLANES)
        table = jnp.maximum(i - 1, 0) * tm + first_row
        for q in range(count):
            dst = jnp.where(i == 0, dump_base + first_row + q, inv_ref[table + q])
            row_copy(prev, first_row + q, dst).start()

    @pl.when((i == 0) & (j == 0))
    def _():
        yo_ref[...] = jnp.zeros(yo_ref.shape, F32)

    def sends_rows(t):
        return (t == 0) | (tr_ref[jnp.maximum(t - 1, 0)] > 0)

    @pl.when((i > 0) & (j == 0) & sends_rows(i - 1))
    def _():
        wait_rows(slot)

    for n in range(1, tm // sub + 1):
        m = n * sub

        @pl.when(n_live == n)
        def _():
            send_prev_rows(j * chunk, chunk)
            x = x_ref[:m, :].astype(BF16)
            part = _swiglu_chunks(x, wg_ref.at[0], wu_ref.at[0], wd_ref.at[0], MXU_COLS)
            yo_ref[slot, :m, :] = jnp.where(j > 0, yo_ref[slot, :m, :], 0.0) + part

            @pl.when(j == 0)
            def _():
                send_prev_rows(n_f * chunk, rest)
                if m < tm:
                    yo_ref[slot, m:, :] = jnp.zeros((tm - m, D), F32)

    @pl.when((n_live == 0) & sends_rows(i))
    def _():
        send_prev_rows(j * chunk, chunk)

        @pl.when(j == 0)
        def _():
            send_prev_rows(n_f * chunk, rest)

    @pl.when((i == pl.num_programs(0) - 1) & (j == n_f - 1) & sends_rows(i))
    def _():
        wait_rows(prev)


def _experts(xs, inv, tile_expert, tile_rows, tile_src, w_gate, w_up, w_down, n_tokens):
    P = xs.shape[0]
    E, D, F = w_gate.shape
    tm, tf = EXPERT_TILE, EXPERT_F_TILE
    n_f = F // tf
    dump_base = TOP_K * n_tokens
    dump_rows = tm

    def row_map(i, j, te, tr, tx, inv):
        return (tx[i], 0)

    def col_w(i, j, te, tr, tx, inv):
        return (te[i], 0, jnp.where(tr[i] > 0, j, n_f - 1))

    def row_w(i, j, te, tr, tx, inv):
        return (te[i], jnp.where(tr[i] > 0, j, n_f - 1), 0)

    grid_spec = pltpu.PrefetchScalarGridSpec(
        num_scalar_prefetch=4,
        grid=(P // tm + 1, n_f),
        in_specs=[pl.BlockSpec((tm, D), row_map),
                  pl.BlockSpec((1, D, tf), col_w),
                  pl.BlockSpec((1, D, tf), col_w),
                  pl.BlockSpec((1, tf, D), row_w)],
        out_specs=pl.BlockSpec(memory_space=pl.ANY),
        scratch_shapes=[pltpu.VMEM((2, tm, D), F32), pltpu.SemaphoreType.DMA((2,))],
    )
    return pl.pallas_call(
        functools.partial(_expert_kernel, n_f=n_f, dump_base=dump_base),
        grid_spec=grid_spec,
        out_shape=jax.ShapeDtypeStruct((dump_base + dump_rows, D), F32),
        compiler_params=_params("arbitrary", "arbitrary"),
        name="expert_swiglu",
    )(tile_expert, tile_rows, tile_src, inv, xs, w_gate, w_up, w_down)


def _combine_kernel(h_ref, y1_ref, y2_ref, info_ref, p_ref, gp_ref, wpg_ref, wpp_ref, o_ref):
    info = info_ref[...]
    h = h_ref[...] + info[:, 2:3] * y1_ref[...] + info[:, 3:4] * y2_ref[...]
    o_ref[...] = _ple(h, p_ref[...], gp_ref[...], wpg_ref[...], wpp_ref[...])


def _combine(h, yk, info, p3, layer, g_ple, w_pg, w_pp):
    T, D = h.shape
    PD = p3.shape[2]
    tm = COMBINE_TILE
    nb = T // tm
    row = lambda w: pl.BlockSpec((tm, w), lambda i: (i, 0))
    return pl.pallas_call(
        _combine_kernel,
        grid=(nb,),
        in_specs=[row(D), row(D), pl.BlockSpec((tm, D), lambda i: (nb + i, 0)), row(LANES),
                  pl.BlockSpec((None, tm, PD), lambda i: (layer, i, 0)),
                  _const_spec((1, D)), _const_spec((D, D)), _const_spec((PD, D))],
        out_specs=row(D),
        out_shape=jax.ShapeDtypeStruct((T, D), F32),
        compiler_params=_params("parallel"),
        name="combine_ple",
    )(h, yk, yk, info, p3, g_ple, w_pg, w_pp)


def _rope_tables(S):
    inv = 1.0 / (ROPE_THETA ** (jnp.arange(0, DH, 2, dtype=F32) / DH))
    ang = jnp.arange(S, dtype=F32)[:, None] * inv[None, :]
    cos, sin = jnp.cos(ang), jnp.sin(ang)
    reps = HEAD_W // DH
    cos_t = jnp.tile(jnp.concatenate([cos, cos], axis=1), (1, reps))
    sin_t = jnp.tile(jnp.concatenate([-sin, sin], axis=1), (1, reps))
    return cos_t, sin_t


def _routing_tables(info, counts_f, n_experts, T):
    tm = EXPERT_TILE
    e12 = info[:, 0:2].astype(jnp.int32)
    rank = info[:, 4:6].astype(jnp.int32)
    counts = counts_f[0, :n_experts].astype(jnp.int32)
    padded = (counts + tm - 1) // tm * tm
    ends = jnp.cumsum(padded)
    starts = ends - padded
    onehot = e12[:, :, None] == jnp.arange(n_experts, dtype=jnp.int32)
    pos = jnp.sum(jnp.where(onehot, starts, 0), axis=-1) + rank
    P = T * TOP_K + n_experts * tm
    sub = EXPERT_SUB
    pad = jnp.stack([starts + counts, (counts + sub - 1) // sub * sub - counts])
    blk_start = jnp.arange(P // sub, dtype=jnp.int32) * sub
    holds_rows = jnp.any((blk_start[:, None] >= starts[None, :]) & (blk_start[:, None] < (starts + counts)[None, :]), axis=1)
    slot = jnp.where(holds_rows, -1, jnp.cumsum(~holds_rows) - 1)
    hit = slot[None, :] == jnp.arange(n_experts * tm // sub, dtype=jnp.int32)[:, None]
    empty_blocks = (jnp.sum(jnp.where(hit, jnp.arange(P // sub, dtype=jnp.int32) + 1, 0), axis=1) - 1).astype(jnp.int32)
    n_tiles = P // tm + 1
    tile_start = jnp.arange(n_tiles, dtype=jnp.int32) * tm
    tile_expert = jnp.minimum(jnp.sum(tile_start[:, None] >= ends[None, :], axis=1), n_experts - 1).astype(jnp.int32)
    tile_rows = jnp.clip((starts + counts)[tile_expert] - tile_start, 0, tm).astype(jnp.int32)
    tile_src = jnp.minimum(jnp.arange(n_tiles, dtype=jnp.int32), ends[-1] // tm - 1).astype(jnp.int32)
    return pos, pad, empty_blocks, P, tile_expert, tile_rows, tile_src


def kernel(x, p, norm_mix, norm_ffn, norm_ple, attn_w_qkv, attn_w_o, attn_q_norm, attn_k_norm,
           attn_lambda_q1, attn_lambda_k1, attn_lambda_q2, attn_lambda_k2, attn_subln,
           pool_w, pool_b, pool_scale, ffn_w_gate, ffn_w_up, ffn_w_down,
           moe_router, moe_w_gate, moe_w_up, moe_w_down, ple_w_proj, ple_w_gate):
    B, S, D = x.shape
    T = B * S
    n_experts = moe_router.shape[-1]
    x2 = x.reshape(T, D)
    p3 = p.reshape(p.shape[0], T, p.shape[-1])
    bf = lambda a: a.astype(BF16)
    vec = lambda a: a.reshape(1, -1)

    lam_init = 0.8 - 0.6 * math.exp(-0.3 * 0)
    cos_t, sin_t = _rope_tables(S)
    reps = HEAD_W // DH
    qk_gain = jnp.zeros((8, LANES), F32)
    qk_gain = qk_gain.at[0].set(jnp.tile(attn_q_norm[0], reps) * (ATTN_SCALE * math.log2(math.e)))
    qk_gain = qk_gain.at[1].set(jnp.tile(attn_k_norm[0], reps))
    group = jnp.arange(2 * LANES) // DH
    ones = (group[:, None] == group[None, :]).astype(BF16)
    qkv = _qkv(x2, vec(norm_mix[0]), attn_w_qkv[0], cos_t, sin_t, qk_gain, ones)
    lam_params = jnp.stack([attn_lambda_q1[0], attn_lambda_k1[0], attn_lambda_q2[0], attn_lambda_k2[0]])
    ao = _attention(qkv.reshape(B, S, 3 * D), lam_params, vec(attn_subln[0]), lam_init)
    h = _dense_tail(ao.reshape(T, D), x2, p3, 0, bf(attn_w_o[0]), vec(norm_ffn[0]),
                    bf(ffn_w_gate[0]), bf(ffn_w_up[0]), bf(ffn_w_down[0]),
                    vec(norm_ple[0]), bf(ple_w_gate[0]), bf(ple_w_proj[0]))

    w_router = jnp.zeros((D, LANES), F32).at[:, :n_experts].set(moe_router[0])
    w_router_hi = w_router.astype(BF16)
    w_router_pad = jnp.stack([w_router_hi, (w_router - w_router_hi.astype(F32)).astype(BF16)])
    h, xn, info, counts = _pool_router(h, S, vec(norm_mix[1]), bf(pool_w[0]), vec(pool_b[0]),
                                       vec(pool_scale[0]), vec(norm_ffn[1]), w_router_pad, n_experts)
    pos, pad, empty_blocks, n_sorted, tile_expert, tile_rows, tile_src = _routing_tables(info, counts, n_experts, T)
    xs, inv = _row_scatter(xn, pos, pad, empty_blocks, n_sorted)
    yk = _experts(xs, inv, tile_expert, tile_rows, tile_src, moe_w_gate[0], moe_w_up[0], moe_w_down[0], T)
    out = _combine(h, yk, info, p3, 1, vec(norm_ple[1]), bf(ple_w_gate[1]), bf(ple_w_proj[1]))
    return out.reshape(B, S, D)
```

```python
import functools
import math

import jax
import jax.numpy as jnp
from jax import lax
from jax.experimental import pallas as pl
from jax.experimental.pallas import tpu as pltpu

F32 = jnp.float32
BF16 = jnp.bfloat16

DH = 64
HEAD_W = 2 * DH
ATTN_SCALE = 1.0 / math.sqrt(DH)
ROPE_THETA = 10000.0
POOL_WINDOWS = (2, 4, 8, 16)
POOL_HALO = 32
TOP_K = 2
RMS_EPS = 1e-6
LANES = 128
SUBLANES = 8
MXU_COLS = 256
DENSE_F_CHUNK = 512
VMEM_LIMIT = 56 * 1024 * 1024

ROW_TILE = 512
COMBINE_TILE = 1024
Q_TILE = 256
ATTN_HEADS_PER_STEP = 2
EXPERT_TILE = 1024
EXPERT_SUB = 256
EXPERT_F_TILE = 512
GATHER_ROWS = 1024


def _rms(x, gain):
    return x * lax.rsqrt(jnp.mean(x * x, axis=-1, keepdims=True) + RMS_EPS) * gain


def _const_spec(shape):
    zeros = (0,) * len(shape)
    return pl.BlockSpec(shape, lambda *_: zeros, pipeline_mode=pl.Buffered(1))


def _params(*sem):
    return pltpu.CompilerParams(dimension_semantics=sem, vmem_limit_bytes=VMEM_LIMIT)


def _qkv_kernel(x_ref, g_ref, w_ref, cos_ref, sin_ref, qkg_ref, ones_ref, o_ref, wb_ref):
    D = x_ref.shape[1]

    @pl.when(pl.program_id(0) == 0)
    def _():
        wb_ref[...] = w_ref[...].astype(BF16)

    xn = _rms(x_ref[...], g_ref[...]).astype(BF16)
    cos = cos_ref[...]
    sin = sin_ref[...]
    lane = lax.broadcasted_iota(jnp.int32, cos.shape, 1)
    first_half = (lane % DH) < (DH // 2)
    ones = ones_ref[...]
    width = ones.shape[0]
    def project(col):
        return jnp.dot(xn, wb_ref[:, col:col + width], preferred_element_type=F32)

    def finish(col, yc):
        part = col // D
        if part == 2:
            o_ref[:, col:col + width] = yc.astype(BF16)
            return
        gain = qkg_ref[part:part + 1, :]
        ss = jnp.dot((yc * yc).astype(BF16), ones, preferred_element_type=F32)
        yn = yc * lax.rsqrt(ss * (1.0 / DH) + RMS_EPS)
        for s in range(0, width, LANES):
            z = yn[:, s:s + LANES] * gain
            rot = jnp.where(first_half,
                            pltpu.roll(z, LANES - DH // 2, 1),
                            pltpu.roll(z, DH // 2, 1))
            o_ref[:, col + s:col + s + LANES] = (z * cos + rot * sin).astype(BF16)

    cols = list(range(0, 3 * D, width))
    pending = project(cols[0])
    for n, col in enumerate(cols):
        upcoming = project(cols[n + 1]) if n + 1 < len(cols) else None
        finish(col, pending)
        pending = upcoming


def _qkv(x2, gain, w_qkv, cos_t, sin_t, qk_gain, ones):
    T, D = x2.shape
    S = cos_t.shape[0]
    tm = ROW_TILE
    blocks_per_seq = S // tm
    return pl.pallas_call(
        _qkv_kernel,
        grid=(T // tm,),
        in_specs=[
            pl.BlockSpec((tm, D), lambda i: (i, 0)),
            _const_spec((1, D)),
            _const_spec((D, 3 * D)),
            pl.BlockSpec((tm, LANES), lambda i: (i % blocks_per_seq, 0)),
            pl.BlockSpec((tm, LANES), lambda i: (i % blocks_per_seq, 0)),
            _const_spec(qk_gain.shape),
            _const_spec(ones.shape),
        ],
        out_specs=pl.BlockSpec((tm, 3 * D), lambda i: (i, 0)),
        out_shape=jax.ShapeDtypeStruct((T, 3 * D), BF16),
        scratch_shapes=[pltpu.VMEM((D, 3 * D), BF16)],
        compiler_params=_params("arbitrary"),
        name="qkv_rope",
    )(x2, gain, w_qkv, cos_t, sin_t, qk_gain, ones)


def _dot_nt(a, b):
    return lax.dot_general(a, b, (((1,), (1,)), ((), ())), preferred_element_type=F32)


def _attn_kernel(q_ref, k_ref, v_ref, lam_ref, g_ref, o_ref, vext_ref, *, lam_init):
    lp = lam_ref[...]
    lam = (jnp.exp(jnp.sum(lp[0:1] * lp[1:2], axis=-1, keepdims=True))
           - jnp.exp(jnp.sum(lp[2:3] * lp[3:4], axis=-1, keepdims=True)) + lam_init)
    S = q_ref.shape[1]
    tq = Q_TILE
    vlane = lax.broadcasted_iota(jnp.int32, (S, HEAD_W), 1)
    lane = lax.broadcasted_iota(jnp.int32, (tq, HEAD_W), 1)
    row = lax.broadcasted_iota(jnp.int32, (tq, tq), 0)
    col = lax.broadcasted_iota(jnp.int32, (tq, tq), 1)
    future = col > row
    gain = g_ref[...]
    n_heads = q_ref.shape[2] // HEAD_W
    for head in range(n_heads):
        vext_ref[head, :, :HEAD_W] = v_ref[0, :, head * HEAD_W:(head + 1) * HEAD_W]
        vext_ref[head, :, HEAD_W:] = jnp.where(vlane == 0, 1.0, 0.0).astype(BF16)

    def scores(head, i):
        cols = slice(head * HEAD_W, (head + 1) * HEAD_W)
        q = q_ref[0, i * tq:(i + 1) * tq, cols]
        out = []
        for qc in (jnp.where(lane < DH, q, jnp.zeros_like(q)), jnp.where(lane >= DH, q, jnp.zeros_like(q))):
            s_diag = jnp.where(future, -jnp.inf, _dot_nt(qc, k_ref[0, i * tq:(i + 1) * tq, cols]))
            s_off = _dot_nt(qc, k_ref[0, :i * tq, cols]) if i > 0 else None
            out.append((s_off, s_diag))
        return out

    def finish(head, i, maps):
        kv = (i + 1) * tq
        acc = []
        for s_off, s_diag in maps:
            m = jnp.max(s_diag, axis=-1, keepdims=True)
            if s_off is not None:
                m = jnp.maximum(m, jnp.max(s_off, axis=-1, keepdims=True))
                p = jnp.concatenate([jnp.exp2(s_off - m).astype(BF16),
                                     jnp.exp2(s_diag - m).astype(BF16)], axis=1)
            else:
                p = jnp.exp2(s_diag - m).astype(BF16)
            acc.append(jnp.dot(p, vext_ref[head, :kv, :], preferred_element_type=F32))
        o = (acc[0][:, :HEAD_W] * (1.0 / acc[0][:, HEAD_W:HEAD_W + 1])
             - acc[1][:, :HEAD_W] * (lam / acc[1][:, HEAD_W:HEAD_W + 1]))
        o = _rms(o, gain) * (1.0 - lam_init)
        o_ref[0, i * tq:(i + 1) * tq, head * HEAD_W:(head + 1) * HEAD_W] = o.astype(BF16)

    work = [(head, i) for head in range(n_heads) for i in range(S // tq)]
    pending = scores(*work[0])
    for pos, item in enumerate(work):
        upcoming = scores(*work[pos + 1]) if pos + 1 < len(work) else None
        finish(*item, pending)
        pending = upcoming


def _attention(qkv3, lam_params, subln, lam_init):
    B, S, W3 = qkv3.shape
    heads = ATTN_HEADS_PER_STEP
    H = W3 // 3 // (heads * HEAD_W)
    blk = (1, S, heads * HEAD_W)
    return pl.pallas_call(
        functools.partial(_attn_kernel, lam_init=lam_init),
        grid=(B, H),
        in_specs=[
            pl.BlockSpec(blk, lambda b, h: (b, 0, h)),
            pl.BlockSpec(blk, lambda b, h: (b, 0, H + h)),
            pl.BlockSpec(blk, lambda b, h: (b, 0, 2 * H + h)),
            _const_spec(lam_params.shape),
            _const_spec(subln.shape),
        ],
        out_specs=pl.BlockSpec(blk, lambda b, h: (b, 0, h)),
        out_shape=jax.ShapeDtypeStruct((B, S, W3 // 3), BF16),
        scratch_shapes=[pltpu.VMEM((heads, S, 2 * HEAD_W), BF16)],
        compiler_params=_params("parallel", "parallel"),
        name="diff_attn",
    )(qkv3, qkv3, qkv3, lam_params, subln)


def _ple(h, p, g_ple, w_gate, w_proj):
    xn = _rms(h, g_ple).astype(BF16)
    gate = jax.nn.sigmoid(jnp.dot(xn, w_gate, preferred_element_type=F32))
    proj = jnp.dot(p.astype(BF16), w_proj, preferred_element_type=F32)
    return h + gate * proj


def _swiglu_chunks(xn, wg_ref, wu_ref, wd_ref, width):
    F = wg_ref.shape[1]
    chunks = [(lo, min(lo + width, F)) for lo in range(0, F, width)]

    def gate_up(lo, hi):
        return (jnp.dot(xn, wg_ref[:, lo:hi].astype(BF16), preferred_element_type=F32),
                jnp.dot(xn, wu_ref[:, lo:hi].astype(BF16), preferred_element_type=F32))

    out = None
    pending = gate_up(*chunks[0])
    for c, (lo, hi) in enumerate(chunks):
        upcoming = gate_up(*chunks[c + 1]) if c + 1 < len(chunks) else None
        g, u = pending
        act = (g * jax.nn.sigmoid(g) * u).astype(BF16)
        part = jnp.dot(act, wd_ref[lo:hi, :].astype(BF16), preferred_element_type=F32)
        out = part if out is None else out + part
        pending = upcoming
    return out


def _dense_tail_kernel(ao_ref, x_ref, p_ref, wo_ref, gf_ref, wg_ref, wu_ref, wd_ref,
                       gp_ref, wpg_ref, wpp_ref, o_ref, *, f_chunk):
    h = x_ref[...] + jnp.dot(ao_ref[...], wo_ref[...], preferred_element_type=F32)
    xn = _rms(h, gf_ref[...]).astype(BF16)
    acc = h + _swiglu_chunks(xn, wg_ref, wu_ref, wd_ref, f_chunk)
    o_ref[...] = _ple(acc, p_ref[...], gp_ref[...], wpg_ref[...], wpp_ref[...])


def _dense_tail(ao, x2, p3, layer, w_o, g_ffn, w_gate, w_up, w_down, g_ple, w_pg, w_pp):
    T, D = x2.shape
    F = w_gate.shape[1]
    PD = p3.shape[2]
    tm = ROW_TILE
    row = lambda w: pl.BlockSpec((tm, w), lambda i: (i, 0))
    return pl.pallas_call(
        functools.partial(_dense_tail_kernel, f_chunk=DENSE_F_CHUNK),
        grid=(T // tm,),
        in_specs=[row(D), row(D), pl.BlockSpec((None, tm, PD), lambda i: (layer, i, 0)),
                  _const_spec((D, D)), _const_spec((1, D)),
                  _const_spec((D, F)), _const_spec((D, F)), _const_spec((F, D)),
                  _const_spec((1, D)), _const_spec((D, D)), _const_spec((PD, D))],
        out_specs=row(D),
        out_shape=jax.ShapeDtypeStruct((T, D), F32),
        compiler_params=_params("parallel"),
        name="dense_tail",
    )(ao, x2, p3, w_o, g_ffn, w_gate, w_up, w_down, g_ple, w_pg, w_pp)


def _pool_router_kernel(h_ref, halo_ref, gm_ref, pw_ref, pb_ref, ps_ref, gf_ref, wr_ref,
                        h_out_ref, xn_out_ref, info_ref, cnt_ref, xbuf_ref, carry_ref, *stage_refs,
                        blocks_per_seq, n_experts):
    i = pl.program_id(0)
    tm, D = h_ref.shape
    gw = D // len(POOL_WINDOWS)
    seq_block = i % blocks_per_seq

    @pl.when(i == 0)
    def _():
        carry_ref[...] = jnp.zeros_like(carry_ref)

    h = h_ref[...]
    gm = gm_ref[...]
    xbuf_ref[POOL_HALO:, :] = _rms(h, gm)

    @pl.when(seq_block == 0)
    def _():
        xbuf_ref[:POOL_HALO, :] = jnp.zeros((POOL_HALO, D), F32)

    @pl.when(seq_block != 0)
    def _():
        xbuf_ref[:POOL_HALO, :] = _rms(halo_ref[...], gm)

    n_rows = tm + POOL_HALO
    src, lo = xbuf_ref, 0
    windows = []
    for g, w in enumerate(POOL_WINDOWS):
        shift = w // 2
        assert w == 2 << g
        lo = -(-(lo + shift) // SUBLANES) * SUBLANES
        assert lo <= POOL_HALO
        val = src[lo:n_rows, g * gw:] + src[lo - shift:n_rows - shift, g * gw:]
        if g + 1 < len(POOL_WINDOWS):
            dst = stage_refs[g % 2]
            dst[lo:n_rows, g * gw:] = val
            windows.append(dst[POOL_HALO:n_rows, g * gw:(g + 1) * gw])
            src = dst
        else:
            windows.append(val[POOL_HALO - lo:, :])

    t = seq_block * tm + lax.broadcasted_iota(jnp.int32, (tm, 1), 0)
    pieces = []
    for g, w in enumerate(POOL_WINDOWS):
        tok = xbuf_ref[POOL_HALO:, g * gw:(g + 1) * gw]
        cnt = jnp.minimum(t + 1, w).astype(F32)
        y = (windows[g] / cnt - tok).astype(BF16)
        pieces.append(jnp.dot(y, pw_ref[g], preferred_element_type=F32))
    mix = (jnp.concatenate(pieces, axis=1) + pb_ref[...]) * ps_ref[...]
    h2 = h + mix
    h_out_ref[...] = h2
    xn = _rms(h2, gf_ref[...])
    xn_out_ref[...] = xn

    xh = xn.astype(BF16)
    xl = (xn - xh.astype(F32)).astype(BF16)
    logits = (jnp.dot(xh, wr_ref[0], preferred_element_type=F32)
              + jnp.dot(xl, wr_ref[0], preferred_element_type=F32)
              + jnp.dot(xh, wr_ref[1], preferred_element_type=F32))
    lane = lax.broadcasted_iota(jnp.int32, logits.shape, 1)
    lg = jnp.where(lane < n_experts, logits, -jnp.inf)
    m1 = jnp.max(lg, axis=-1, keepdims=True)
    i1 = jnp.min(jnp.where(lg == m1, lane, LANES), axis=-1, keepdims=True)
    lg2 = jnp.where(lane == i1, -jnp.inf, lg)
    m2 = jnp.max(lg2, axis=-1, keepdims=True)
    i2 = jnp.min(jnp.where(lg2 == m2, lane, LANES), axis=-1, keepdims=True)
    e = jnp.exp(m2 - m1)
    w1 = 1.0 / (1.0 + e)
    w2 = e / (1.0 + e)

    sel1 = lane == i1
    sel2 = lane == i2
    onehot = jnp.where(sel1 | sel2, 1.0, 0.0)
    r = lax.broadcasted_iota(jnp.int32, (tm, tm), 0)
    c = lax.broadcasted_iota(jnp.int32, (tm, tm), 1)
    strict_lower = jnp.where(c < r, 1.0, 0.0).astype(BF16)
    before = jnp.dot(strict_lower, onehot.astype(BF16), preferred_element_type=F32) + carry_ref[0:1, :]
    rank1 = jnp.sum(jnp.where(sel1, before, 0.0), axis=-1, keepdims=True)
    rank2 = jnp.sum(jnp.where(sel2, before, 0.0), axis=-1, keepdims=True)
    total = carry_ref[0:1, :] + jnp.sum(onehot, axis=0, keepdims=True)
    carry_ref[...] = jnp.broadcast_to(total, carry_ref.shape)
    cnt_ref[...] = jnp.broadcast_to(total, cnt_ref.shape)

    info = jnp.where(lane == 0, i1.astype(F32), 0.0)
    info = jnp.where(lane == 1, i2.astype(F32), info)
    info = jnp.where(lane == 2, w1, info)
    info = jnp.where(lane == 3, w2, info)
    info = jnp.where(lane == 4, rank1, info)
    info = jnp.where(lane == 5, rank2, info)
    info_ref[...] = info


def _pool_router(h, S, g_mix, pool_w, pool_b, pool_scale, g_ffn, w_router_pad, n_experts):
    T, D = h.shape
    tm = ROW_TILE
    G, gw, _ = pool_w.shape
    halo_per_tile = tm // POOL_HALO
    row = pl.BlockSpec((tm, D), lambda i: (i, 0))
    return pl.pallas_call(
        functools.partial(_pool_router_kernel, blocks_per_seq=S // tm, n_experts=n_experts),
        grid=(T // tm,),
        in_specs=[
            row,
            pl.BlockSpec((POOL_HALO, D), lambda i: (jnp.maximum(i * halo_per_tile - 1, 0), 0)),
            _const_spec((1, D)), _const_spec((G, gw, gw)), _const_spec((1, D)), _const_spec((1, D)),
            _const_spec((1, D)), _const_spec((2, D, LANES)),
        ],
        out_specs=[row, row,
                   pl.BlockSpec((tm, LANES), lambda i: (i, 0)),
                   pl.BlockSpec((8, LANES), lambda i: (0, 0))],
        out_shape=[jax.ShapeDtypeStruct((T, D), F32), jax.ShapeDtypeStruct((T, D), F32),
                   jax.ShapeDtypeStruct((T, LANES), F32), jax.ShapeDtypeStruct((8, LANES), F32)],
        scratch_shapes=[pltpu.VMEM((tm + POOL_HALO, D), F32), pltpu.VMEM((8, LANES), F32),
                        pltpu.VMEM((tm + POOL_HALO, D), F32), pltpu.VMEM((tm + POOL_HALO, D), F32)],
        compiler_params=_params("arbitrary"),
        name="pool_router",
    )(h, h, g_mix, pool_w, pool_b, pool_scale, g_ffn, w_router_pad)


def _scatter_kernel(pos_ref, pad_ref, blk_ref, inv_init_ref, x_ref, out_ref, inv_ref, sem):
    n = x_ref.shape[0]
    n_tokens = n * pl.num_programs(0)
    first = pl.program_id(0) * n

    def copy(src_row, n_rows, dst_row):
        return pltpu.make_async_copy(x_ref.at[pl.ds(src_row, n_rows)], out_ref.at[pl.ds(dst_row, n_rows)], sem)

    def issue(group, carry):
        r0 = pl.multiple_of(group * SUBLANES, SUBLANES)
        for q in range(SUBLANES):
            for k in range(TOP_K):
                p = pos_ref[0, k, r0 + q]
                copy(r0 + q, 1, p).start()
                inv_ref[EXPERT_TILE + p] = k * n_tokens + first + r0 + q
        return carry

    lax.fori_loop(0, n // SUBLANES, issue, 0)

    @pl.when(pl.program_id(0) == pl.num_programs(0) - 1)
    def _():
        def keep_lead(r, carry):
            inv_ref[r] = inv_init_ref[r]
            return carry

        lax.fori_loop(0, EXPERT_TILE, keep_lead, 0, unroll=8)

        for e in range(pad_ref.shape[1]):
            start = pad_ref[0, e]
            count = pad_ref[1, e]

            def fill(r, carry):
                copy(0, 1, start + r).start()
                inv_ref[EXPERT_TILE + start + r] = inv_init_ref[EXPERT_TILE + start + r]
                return carry

            lax.fori_loop(0, count, fill, 0)

            whole = pl.multiple_of(count // 8 * 8, 8)

            @pl.when(whole > 0)
            def _():
                copy(0, whole, 0).wait()

            def drain(r, carry):
                copy(0, 1, 0).wait()
                return carry

            lax.fori_loop(0, count - whole, drain, 0)

        for k in range(blk_ref.shape[0]):
            blk = blk_ref[k]

            @pl.when(blk >= 0)
            def _():
                c = copy(0, EXPERT_SUB, pl.multiple_of(blk * EXPERT_SUB, EXPERT_SUB))
                c.start()

                def keep(r, carry):
                    inv_ref[EXPERT_TILE + blk * EXPERT_SUB + r] = inv_init_ref[EXPERT_TILE + blk * EXPERT_SUB + r]
                    return carry

                lax.fori_loop(0, EXPERT_SUB, keep, 0, unroll=8)
                c.wait()

    for k in range(TOP_K):
        copy(0, n, 0).wait()


def _row_scatter(x, pos, pad, empty_blocks, n_out):
    T, D = x.shape
    n = GATHER_ROWS
    assert EXPERT_SUB <= n
    pos_blocks = pos.reshape(T // n, n, TOP_K).transpose(0, 2, 1)
    pad_dst = TOP_K * T + jnp.arange(n_out + EXPERT_TILE, dtype=jnp.int32) % EXPERT_TILE
    return pl.pallas_call(
        _scatter_kernel,
        grid=(T // n,),
        in_specs=[pl.BlockSpec((1, TOP_K, n), lambda i: (i, 0, 0), memory_space=pltpu.SMEM),
                  pl.BlockSpec(memory_space=pltpu.SMEM),
                  pl.BlockSpec(memory_space=pltpu.SMEM),
                  pl.BlockSpec(memory_space=pltpu.SMEM),
                  pl.BlockSpec((n, D), lambda i: (i, 0))],
        out_specs=[pl.BlockSpec(memory_space=pl.ANY), pl.BlockSpec(memory_space=pltpu.SMEM)],
        out_shape=[jax.ShapeDtypeStruct((n_out, D), x.dtype),
                   jax.ShapeDtypeStruct((n_out + EXPERT_TILE,), jnp.int32)],
        scratch_shapes=[pltpu.SemaphoreType.DMA],
        compiler_params=_params("arbitrary"),
        name="row_scatter",
    )(pos_blocks, pad, empty_blocks, pad_dst, x)


def _expert_kernel(te_ref, tr_ref, tx_ref, inv_ref, x_ref, wg_ref, wu_ref, wd_ref, yk_ref, yo_ref, sem,
                   *, n_f):
    i = pl.program_id(0)
    j = pl.program_id(1)
    tm, D = x_ref.shape
    sub = EXPERT_SUB
    chunk = tm // n_f // SUBLANES * SUBLANES
    rest = tm - n_f * chunk
    slot = i % 2
    prev = 1 - slot
    n_live = (tr_ref[i] + sub - 1) // sub

    def row_copy(buf, row, dst):
        return pltpu.make_async_copy(yo_ref.at[buf, pl.ds(row, 1)], yk_ref.at[pl.ds(dst, 1)], sem.at[buf])

    def wait_rows(buf):
        pltpu.make_async_copy(yo_ref.at[buf], yk_ref.at[pl.ds(0, tm)], sem.at[buf]).wait()

    def send_prev_rows(first_row, count):
        first_row = pl.multiple_of(first_row, SUBLANES)
        table = i * tm + first_row
        for q in range(count):
            row_copy(prev, first_row + q, inv_ref[table + q]).start()

    @pl.when((i == 0) & (j == 0))
    def _():
        yo_ref[...] = jnp.zeros(yo_ref.shape, F32)

    def sends_rows(t):
        return (t == 0) | (tr_ref[jnp.maximum(t - 1, 0)] > 0)

    @pl.when((i > 0) & (j == 0) & sends_rows(i - 1))
    def _():
        wait_rows(slot)

    for n in range(1, tm // sub + 1):
        m = n * sub

        @pl.when(n_live == n)
        def _():
            send_prev_rows(j * chunk, chunk)
            x = x_ref[:m, :].astype(BF16)
            part = _swiglu_chunks(x, wg_ref.at[0], wu_ref.at[0], wd_ref.at[0], MXU_COLS)
            yo_ref[slot, :m, :] = jnp.where(j > 0, yo_ref[slot, :m, :], 0.0) + part

            @pl.when(j == 0)
            def _():
                send_prev_rows(n_f * chunk, rest)
                if m < tm:
                    yo_ref[slot, m:, :] = jnp.zeros((tm - m, D), F32)

    @pl.when((n_live == 0) & sends_rows(i))
    def _():
        send_prev_rows(j * chunk, chunk)

        @pl.when(j == 0)
        def _():
            send_prev_rows(n_f * chunk, rest)

    @pl.when((i == pl.num_programs(0) - 1) & (j == n_f - 1) & sends_rows(i))
    def _():
        wait_rows(prev)


def _experts(xs, inv, tile_expert, tile_rows, tile_src, w_gate, w_up, w_down, n_tokens):
    P = xs.shape[0]
    E, D, F = w_gate.shape
    tm, tf = EXPERT_TILE, EXPERT_F_TILE
    n_f = F // tf
    dump_base = TOP_K * n_tokens
    dump_rows = tm

    def row_map(i, j, te, tr, tx, inv):
        return (tx[i], 0)

    def col_w(i, j, te, tr, tx, inv):
        return (te[i], 0, jnp.where(tr[i] > 0, j, n_f - 1))

    def row_w(i, j, te, tr, tx, inv):
        return (te[i], jnp.where(tr[i] > 0, j, n_f - 1), 0)

    grid_spec = pltpu.PrefetchScalarGridSpec(
        num_scalar_prefetch=4,
        grid=(P // tm + 1, n_f),
        in_specs=[pl.BlockSpec((tm, D), row_map),
                  pl.BlockSpec((1, D, tf), col_w),
                  pl.BlockSpec((1, D, tf), col_w),
                  pl.BlockSpec((1, tf, D), row_w)],
        out_specs=pl.BlockSpec(memory_space=pl.ANY),
        scratch_shapes=[pltpu.VMEM((2, tm, D), F32), pltpu.SemaphoreType.DMA((2,))],
    )
    return pl.pallas_call(
        functools.partial(_expert_kernel, n_f=n_f),
        grid_spec=grid_spec,
        out_shape=jax.ShapeDtypeStruct((dump_base + dump_rows, D), F32),
        compiler_params=_params("arbitrary", "arbitrary"),
        name="expert_swiglu",
    )(tile_expert, tile_rows, tile_src, inv, xs, w_gate, w_up, w_down)


def _combine_kernel(h_ref, y1_ref, y2_ref, info_ref, p_ref, gp_ref, wpg_ref, wpp_ref, o_ref):
    info = info_ref[...]
    h = h_ref[...] + info[:, 2:3] * y1_ref[...] + info[:, 3:4] * y2_ref[...]
    o_ref[...] = _ple(h, p_ref[...], gp_ref[...], wpg_ref[...], wpp_ref[...])


def _combine(h, yk, info, p3, layer, g_ple, w_pg, w_pp):
    T, D = h.shape
    PD = p3.shape[2]
    tm = COMBINE_TILE
    nb = T // tm
    row = lambda w: pl.BlockSpec((tm, w), lambda i: (i, 0))
    return pl.pallas_call(
        _combine_kernel,
        grid=(nb,),
        in_specs=[row(D), row(D), pl.BlockSpec((tm, D), lambda i: (nb + i, 0)), row(LANES),
                  pl.BlockSpec((None, tm, PD), lambda i: (layer, i, 0)),
                  _const_spec((1, D)), _const_spec((D, D)), _const_spec((PD, D))],
        out_specs=row(D),
        out_shape=jax.ShapeDtypeStruct((T, D), F32),
        compiler_params=_params("parallel"),
        name="combine_ple",
    )(h, yk, yk, info, p3, g_ple, w_pg, w_pp)


def _rope_tables(S):
    inv = 1.0 / (ROPE_THETA ** (jnp.arange(0, DH, 2, dtype=F32) / DH))
    ang = jnp.arange(S, dtype=F32)[:, None] * inv[None, :]
    cos, sin = jnp.cos(ang), jnp.sin(ang)
    reps = HEAD_W // DH
    cos_t = jnp.tile(jnp.concatenate([cos, cos], axis=1), (1, reps))
    sin_t = jnp.tile(jnp.concatenate([-sin, sin], axis=1), (1, reps))
    return cos_t, sin_t


def _routing_tables(info, counts_f, n_experts, T):
    tm = EXPERT_TILE
    e12 = info[:, 0:2].astype(jnp.int32)
    rank = info[:, 4:6].astype(jnp.int32)
    counts = counts_f[0, :n_experts].astype(jnp.int32)
    padded = (counts + tm - 1) // tm * tm
    ends = jnp.cumsum(padded)
    starts = ends - padded
    onehot = e12[:, :, None] == jnp.arange(n_experts, dtype=jnp.int32)
    pos = jnp.sum(jnp.where(onehot, starts, 0), axis=-1) + rank
    P = T * TOP_K + n_experts * tm
    sub = EXPERT_SUB
    pad = jnp.stack([starts + counts, (counts + sub - 1) // sub * sub - counts])
    blk_start = jnp.arange(P // sub, dtype=jnp.int32) * sub
    holds_rows = jnp.any((blk_start[:, None] >= starts[None, :]) & (blk_start[:, None] < (starts + counts)[None, :]), axis=1)
    slot = jnp.where(holds_rows, -1, jnp.cumsum(~holds_rows) - 1)
    hit = slot[None, :] == jnp.arange(n_experts * tm // sub, dtype=jnp.int32)[:, None]
    empty_blocks = (jnp.sum(jnp.where(hit, jnp.arange(P // sub, dtype=jnp.int32) + 1, 0), axis=1) - 1).astype(jnp.int32)
    n_tiles = P // tm + 1
    tile_start = jnp.arange(n_tiles, dtype=jnp.int32) * tm
    tile_expert = jnp.minimum(jnp.sum(tile_start[:, None] >= ends[None, :], axis=1), n_experts - 1).astype(jnp.int32)
    tile_rows = jnp.clip((starts + counts)[tile_expert] - tile_start, 0, tm).astype(jnp.int32)
    tile_src = jnp.minimum(jnp.arange(n_tiles, dtype=jnp.int32), ends[-1] // tm - 1).astype(jnp.int32)
    return pos, pad, empty_blocks, P, tile_expert, tile_rows, tile_src


def kernel(x, p, norm_mix, norm_ffn, norm_ple, attn_w_qkv, attn_w_o, attn_q_norm, attn_k_norm,
           attn_lambda_q1, attn_lambda_k1, attn_lambda_q2, attn_lambda_k2, attn_subln,
           pool_w, pool_b, pool_scale, ffn_w_gate, ffn_w_up, ffn_w_down,
           moe_router, moe_w_gate, moe_w_up, moe_w_down, ple_w_proj, ple_w_gate):
    B, S, D = x.shape
    T = B * S
    n_experts = moe_router.shape[-1]
    x2 = x.reshape(T, D)
    p3 = p.reshape(p.shape[0], T, p.shape[-1])
    bf = lambda a: a.astype(BF16)
    vec = lambda a: a.reshape(1, -1)

    lam_init = 0.8 - 0.6 * math.exp(-0.3 * 0)
    cos_t, sin_t = _rope_tables(S)
    reps = HEAD_W // DH
    qk_gain = jnp.zeros((8, LANES), F32)
    qk_gain = qk_gain.at[0].set(jnp.tile(attn_q_norm[0], reps) * (ATTN_SCALE * math.log2(math.e)))
    qk_gain = qk_gain.at[1].set(jnp.tile(attn_k_norm[0], reps))
    group = jnp.arange(2 * LANES) // DH
    ones = (group[:, None] == group[None, :]).astype(BF16)
    qkv = _qkv(x2, vec(norm_mix[0]), attn_w_qkv[0], cos_t, sin_t, qk_gain, ones)
    lam_params = jnp.stack([attn_lambda_q1[0], attn_lambda_k1[0], attn_lambda_q2[0], attn_lambda_k2[0]])
    ao = _attention(qkv.reshape(B, S, 3 * D), lam_params, vec(attn_subln[0]), lam_init)
    h = _dense_tail(ao.reshape(T, D), x2, p3, 0, bf(attn_w_o[0]), vec(norm_ffn[0]),
                    bf(ffn_w_gate[0]), bf(ffn_w_up[0]), bf(ffn_w_down[0]),
                    vec(norm_ple[0]), bf(ple_w_gate[0]), bf(ple_w_proj[0]))

    w_router = jnp.zeros((D, LANES), F32).at[:, :n_experts].set(moe_router[0])
    w_router_hi = w_router.astype(BF16)
    w_router_pad = jnp.stack([w_router_hi, (w_router - w_router_hi.astype(F32)).astype(BF16)])
    h, xn, info, counts = _pool_router(h, S, vec(norm_mix[1]), bf(pool_w[0]), vec(pool_b[0]),
                                       vec(pool_scale[0]), vec(norm_ffn[1]), w_router_pad, n_experts)
    pos, pad, empty_blocks, n_sorted, tile_expert, tile_rows, tile_src = _routing_tables(info, counts, n_experts, T)
    xs, inv = _row_scatter(xn, pos, pad, empty_blocks, n_sorted)
    yk = _experts(xs, inv, tile_expert, tile_rows, tile_src, moe_w_gate[0], moe_w_up[0], moe_w_down[0], T)
    out = _combine(h, yk, info, p3, 1, vec(norm_ple[1]), bf(ple_w_gate[1]), bf(ple_w_proj[1]))
    return out.reshape(B, S, D)
```

```python
import functools
import math

import jax
import jax.numpy as jnp
from jax import lax
from jax.experimental import pallas as pl
from jax.experimental.pallas import tpu as pltpu

F32 = jnp.float32
BF16 = jnp.bfloat16

DH = 64
HEAD_W = 2 * DH
ATTN_SCALE = 1.0 / math.sqrt(DH)
ROPE_THETA = 10000.0
POOL_WINDOWS = (2, 4, 8, 16)
POOL_HALO = 32
TOP_K = 2
RMS_EPS = 1e-6
LANES = 128
SUBLANES = 8
MXU_COLS = 256
DENSE_F_CHUNK = 512
VMEM_LIMIT = 56 * 1024 * 1024

ROW_TILE = 512
COMBINE_TILE = 1024
Q_TILE = 256
ATTN_HEADS_PER_STEP = 2
EXPERT_TILE = 1024
EXPERT_SUB = 512
EXPERT_F_TILE = 512
GATHER_ROWS = 1024


def _rms(x, gain):
    return x * lax.rsqrt(jnp.mean(x * x, axis=-1, keepdims=True) + RMS_EPS) * gain


def _const_spec(shape):
    zeros = (0,) * len(shape)
    return pl.BlockSpec(shape, lambda *_: zeros, pipeline_mode=pl.Buffered(1))


def _params(*sem):
    return pltpu.CompilerParams(dimension_semantics=sem, vmem_limit_bytes=VMEM_LIMIT)


def _qkv_kernel(x_ref, g_ref, w_ref, cos_ref, sin_ref, qkg_ref, ones_ref, o_ref, wb_ref):
    D = x_ref.shape[1]

    @pl.when(pl.program_id(0) == 0)
    def _():
        wb_ref[...] = w_ref[...].astype(BF16)

    xn = _rms(x_ref[...], g_ref[...]).astype(BF16)
    cos = cos_ref[...]
    sin = sin_ref[...]
    lane = lax.broadcasted_iota(jnp.int32, cos.shape, 1)
    first_half = (lane % DH) < (DH // 2)
    ones = ones_ref[...]
    width = ones.shape[0]
    def project(col):
        return jnp.dot(xn, wb_ref[:, col:col + width], preferred_element_type=F32)

    def finish(col, yc):
        part = col // D
        if part == 2:
            o_ref[:, col:col + width] = yc.astype(BF16)
            return
        gain = qkg_ref[part:part + 1, :]
        ss = jnp.dot((yc * yc).astype(BF16), ones, preferred_element_type=F32)
        yn = yc * lax.rsqrt(ss * (1.0 / DH) + RMS_EPS)
        for s in range(0, width, LANES):
            z = yn[:, s:s + LANES] * gain
            rot = jnp.where(first_half,
                            pltpu.roll(z, LANES - DH // 2, 1),
                            pltpu.roll(z, DH // 2, 1))
            o_ref[:, col + s:col + s + LANES] = (z * cos + rot * sin).astype(BF16)

    cols = list(range(0, 3 * D, width))
    pending = project(cols[0])
    for n, col in enumerate(cols):
        upcoming = project(cols[n + 1]) if n + 1 < len(cols) else None
        finish(col, pending)
        pending = upcoming


def _qkv(x2, gain, w_qkv, cos_t, sin_t, qk_gain, ones):
    T, D = x2.shape
    S = cos_t.shape[0]
    tm = ROW_TILE
    blocks_per_seq = S // tm
    return pl.pallas_call(
        _qkv_kernel,
        grid=(T // tm,),
        in_specs=[
            pl.BlockSpec((tm, D), lambda i: (i, 0)),
            _const_spec((1, D)),
            _const_spec((D, 3 * D)),
            pl.BlockSpec((tm, LANES), lambda i: (i % blocks_per_seq, 0)),
            pl.BlockSpec((tm, LANES), lambda i: (i % blocks_per_seq, 0)),
            _const_spec(qk_gain.shape),
            _const_spec(ones.shape),
        ],
        out_specs=pl.BlockSpec((tm, 3 * D), lambda i: (i, 0)),
        out_shape=jax.ShapeDtypeStruct((T, 3 * D), BF16),
        scratch_shapes=[pltpu.VMEM((D, 3 * D), BF16)],
        compiler_params=_params("arbitrary"),
        name="qkv_rope",
    )(x2, gain, w_qkv, cos_t, sin_t, qk_gain, ones)


def _dot_nt(a, b):
    return lax.dot_general(a, b, (((1,), (1,)), ((), ())), preferred_element_type=F32)


def _attn_kernel(q_ref, k_ref, v_ref, lam_ref, g_ref, o_ref, vext_ref, *, lam_init):
    lp = lam_ref[...]
    lam = (jnp.exp(jnp.sum(lp[0:1] * lp[1:2], axis=-1, keepdims=True))
           - jnp.exp(jnp.sum(lp[2:3] * lp[3:4], axis=-1, keepdims=True)) + lam_init)
    S = q_ref.shape[1]
    tq = Q_TILE
    vlane = lax.broadcasted_iota(jnp.int32, (S, HEAD_W), 1)
    lane = lax.broadcasted_iota(jnp.int32, (tq, HEAD_W), 1)
    row = lax.broadcasted_iota(jnp.int32, (tq, tq), 0)
    col = lax.broadcasted_iota(jnp.int32, (tq, tq), 1)
    future = col > row
    gain = g_ref[...]
    n_heads = q_ref.shape[2] // HEAD_W
    for head in range(n_heads):
        vext_ref[head, :, :HEAD_W] = v_ref[0, :, head * HEAD_W:(head + 1) * HEAD_W]
        vext_ref[head, :, HEAD_W:] = jnp.where(vlane == 0, 1.0, 0.0).astype(BF16)

    def scores(head, i):
        cols = slice(head * HEAD_W, (head + 1) * HEAD_W)
        q = q_ref[0, i * tq:(i + 1) * tq, cols]
        out = []
        for qc in (jnp.where(lane < DH, q, jnp.zeros_like(q)), jnp.where(lane >= DH, q, jnp.zeros_like(q))):
            s_diag = jnp.where(future, -jnp.inf, _dot_nt(qc, k_ref[0, i * tq:(i + 1) * tq, cols]))
            s_off = _dot_nt(qc, k_ref[0, :i * tq, cols]) if i > 0 else None
            out.append((s_off, s_diag))
        return out

    def finish(head, i, maps):
        kv = (i + 1) * tq
        acc = []
        for s_off, s_diag in maps:
            m = jnp.max(s_diag, axis=-1, keepdims=True)
            if s_off is not None:
                m = jnp.maximum(m, jnp.max(s_off, axis=-1, keepdims=True))
                p = jnp.concatenate([jnp.exp2(s_off - m).astype(BF16),
                                     jnp.exp2(s_diag - m).astype(BF16)], axis=1)
            else:
                p = jnp.exp2(s_diag - m).astype(BF16)
            acc.append(jnp.dot(p, vext_ref[head, :kv, :], preferred_element_type=F32))
        o = (acc[0][:, :HEAD_W] * (1.0 / acc[0][:, HEAD_W:HEAD_W + 1])
             - acc[1][:, :HEAD_W] * (lam / acc[1][:, HEAD_W:HEAD_W + 1]))
        o = _rms(o, gain) * (1.0 - lam_init)
        o_ref[0, i * tq:(i + 1) * tq, head * HEAD_W:(head + 1) * HEAD_W] = o.astype(BF16)

    work = [(head, i) for head in range(n_heads) for i in range(S // tq)]
    pending = scores(*work[0])
    for pos, item in enumerate(work):
        upcoming = scores(*work[pos + 1]) if pos + 1 < len(work) else None
        finish(*item, pending)
        pending = upcoming


def _attention(qkv3, lam_params, subln, lam_init):
    B, S, W3 = qkv3.shape
    heads = ATTN_HEADS_PER_STEP
    H = W3 // 3 // (heads * HEAD_W)
    blk = (1, S, heads * HEAD_W)
    return pl.pallas_call(
        functools.partial(_attn_kernel, lam_init=lam_init),
        grid=(B, H),
        in_specs=[
            pl.BlockSpec(blk, lambda b, h: (b, 0, h)),
            pl.BlockSpec(blk, lambda b, h: (b, 0, H + h)),
            pl.BlockSpec(blk, lambda b, h: (b, 0, 2 * H + h)),
            _const_spec(lam_params.shape),
            _const_spec(subln.shape),
        ],
        out_specs=pl.BlockSpec(blk, lambda b, h: (b, 0, h)),
        out_shape=jax.ShapeDtypeStruct((B, S, W3 // 3), BF16),
        scratch_shapes=[pltpu.VMEM((heads, S, 2 * HEAD_W), BF16)],
        compiler_params=_params("parallel", "parallel"),
        name="diff_attn",
    )(qkv3, qkv3, qkv3, lam_params, subln)


def _ple(h, p, g_ple, w_gate, w_proj):
    xn = _rms(h, g_ple).astype(BF16)
    gate = jax.nn.sigmoid(jnp.dot(xn, w_gate, preferred_element_type=F32))
    proj = jnp.dot(p.astype(BF16), w_proj, preferred_element_type=F32)
    return h + gate * proj


def _swiglu_chunks(xn, wg_ref, wu_ref, wd_ref, width):
    F = wg_ref.shape[1]
    chunks = [(lo, min(lo + width, F)) for lo in range(0, F, width)]

    def gate_up(lo, hi):
        return (jnp.dot(xn, wg_ref[:, lo:hi].astype(BF16), preferred_element_type=F32),
                jnp.dot(xn, wu_ref[:, lo:hi].astype(BF16), preferred_element_type=F32))

    out = None
    pending = gate_up(*chunks[0])
    for c, (lo, hi) in enumerate(chunks):
        upcoming = gate_up(*chunks[c + 1]) if c + 1 < len(chunks) else None
        g, u = pending
        act = (g * jax.nn.sigmoid(g) * u).astype(BF16)
        part = jnp.dot(act, wd_ref[lo:hi, :].astype(BF16), preferred_element_type=F32)
        out = part if out is None else out + part
        pending = upcoming
    return out


def _dense_tail_kernel(ao_ref, x_ref, p_ref, wo_ref, gf_ref, wg_ref, wu_ref, wd_ref,
                       gp_ref, wpg_ref, wpp_ref, o_ref, *, f_chunk):
    h = x_ref[...] + jnp.dot(ao_ref[...], wo_ref[...], preferred_element_type=F32)
    xn = _rms(h, gf_ref[...]).astype(BF16)
    acc = h + _swiglu_chunks(xn, wg_ref, wu_ref, wd_ref, f_chunk)
    o_ref[...] = _ple(acc, p_ref[...], gp_ref[...], wpg_ref[...], wpp_ref[...])


def _dense_tail(ao, x2, p3, layer, w_o, g_ffn, w_gate, w_up, w_down, g_ple, w_pg, w_pp):
    T, D = x2.shape
    F = w_gate.shape[1]
    PD = p3.shape[2]
    tm = ROW_TILE
    row = lambda w: pl.BlockSpec((tm, w), lambda i: (i, 0))
    return pl.pallas_call(
        functools.partial(_dense_tail_kernel, f_chunk=DENSE_F_CHUNK),
        grid=(T // tm,),
        in_specs=[row(D), row(D), pl.BlockSpec((None, tm, PD), lambda i: (layer, i, 0)),
                  _const_spec((D, D)), _const_spec((1, D)),
                  _const_spec((D, F)), _const_spec((D, F)), _const_spec((F, D)),
                  _const_spec((1, D)), _const_spec((D, D)), _const_spec((PD, D))],
        out_specs=row(D),
        out_shape=jax.ShapeDtypeStruct((T, D), F32),
        compiler_params=_params("parallel"),
        name="dense_tail",
    )(ao, x2, p3, w_o, g_ffn, w_gate, w_up, w_down, g_ple, w_pg, w_pp)


def _pool_router_kernel(h_ref, halo_ref, gm_ref, pw_ref, pb_ref, ps_ref, gf_ref, wr_ref,
                        h_out_ref, xn_out_ref, info_ref, cnt_ref, xbuf_ref, carry_ref, *stage_refs,
                        blocks_per_seq, n_experts):
    i = pl.program_id(0)
    tm, D = h_ref.shape
    gw = D // len(POOL_WINDOWS)
    seq_block = i % blocks_per_seq

    @pl.when(i == 0)
    def _():
        carry_ref[...] = jnp.zeros_like(carry_ref)

    h = h_ref[...]
    gm = gm_ref[...]
    xbuf_ref[POOL_HALO:, :] = _rms(h, gm)

    @pl.when(seq_block == 0)
    def _():
        xbuf_ref[:POOL_HALO, :] = jnp.zeros((POOL_HALO, D), F32)

    @pl.when(seq_block != 0)
    def _():
        xbuf_ref[:POOL_HALO, :] = _rms(halo_ref[...], gm)

    n_rows = tm + POOL_HALO
    src, lo = xbuf_ref, 0
    windows = []
    for g, w in enumerate(POOL_WINDOWS):
        shift = w // 2
        assert w == 2 << g
        lo = -(-(lo + shift) // SUBLANES) * SUBLANES
        assert lo <= POOL_HALO
        val = src[lo:n_rows, g * gw:] + src[lo - shift:n_rows - shift, g * gw:]
        if g + 1 < len(POOL_WINDOWS):
            dst = stage_refs[g % 2]
            dst[lo:n_rows, g * gw:] = val
            windows.append(dst[POOL_HALO:n_rows, g * gw:(g + 1) * gw])
            src = dst
        else:
            windows.append(val[POOL_HALO - lo:, :])

    t = seq_block * tm + lax.broadcasted_iota(jnp.int32, (tm, 1), 0)
    pieces = []
    for g, w in enumerate(POOL_WINDOWS):
        tok = xbuf_ref[POOL_HALO:, g * gw:(g + 1) * gw]
        cnt = jnp.minimum(t + 1, w).astype(F32)
        y = (windows[g] / cnt - tok).astype(BF16)
        pieces.append(jnp.dot(y, pw_ref[g], preferred_element_type=F32))
    mix = (jnp.concatenate(pieces, axis=1) + pb_ref[...]) * ps_ref[...]
    h2 = h + mix
    h_out_ref[...] = h2
    xn = _rms(h2, gf_ref[...])
    xn_out_ref[...] = xn

    xh = xn.astype(BF16)
    xl = (xn - xh.astype(F32)).astype(BF16)
    logits = (jnp.dot(xh, wr_ref[0], preferred_element_type=F32)
              + jnp.dot(xl, wr_ref[0], preferred_element_type=F32)
              + jnp.dot(xh, wr_ref[1], preferred_element_type=F32))
    lane = lax.broadcasted_iota(jnp.int32, logits.shape, 1)
    lg = jnp.where(lane < n_experts, logits, -jnp.inf)
    m1 = jnp.max(lg, axis=-1, keepdims=True)
    i1 = jnp.min(jnp.where(lg == m1, lane, LANES), axis=-1, keepdims=True)
    lg2 = jnp.where(lane == i1, -jnp.inf, lg)
    m2 = jnp.max(lg2, axis=-1, keepdims=True)
    i2 = jnp.min(jnp.where(lg2 == m2, lane, LANES), axis=-1, keepdims=True)
    e = jnp.exp(m2 - m1)
    w1 = 1.0 / (1.0 + e)
    w2 = e / (1.0 + e)

    sel1 = lane == i1
    sel2 = lane == i2
    onehot = jnp.where(sel1 | sel2, 1.0, 0.0)
    r = lax.broadcasted_iota(jnp.int32, (tm, tm), 0)
    c = lax.broadcasted_iota(jnp.int32, (tm, tm), 1)
    strict_lower = jnp.where(c < r, 1.0, 0.0).astype(BF16)
    before = jnp.dot(strict_lower, onehot.astype(BF16), preferred_element_type=F32) + carry_ref[0:1, :]
    rank1 = jnp.sum(jnp.where(sel1, before, 0.0), axis=-1, keepdims=True)
    rank2 = jnp.sum(jnp.where(sel2, before, 0.0), axis=-1, keepdims=True)
    total = carry_ref[0:1, :] + jnp.sum(onehot, axis=0, keepdims=True)
    carry_ref[...] = jnp.broadcast_to(total, carry_ref.shape)
    cnt_ref[...] = jnp.broadcast_to(total, cnt_ref.shape)

    info = jnp.where(lane == 0, i1.astype(F32), 0.0)
    info = jnp.where(lane == 1, i2.astype(F32), info)
    info = jnp.where(lane == 2, w1, info)
    info = jnp.where(lane == 3, w2, info)
    info = jnp.where(lane == 4, rank1, info)
    info = jnp.where(lane == 5, rank2, info)
    info_ref[...] = info


def _pool_router(h, S, g_mix, pool_w, pool_b, pool_scale, g_ffn, w_router_pad, n_experts):
    T, D = h.shape
    tm = ROW_TILE
    G, gw, _ = pool_w.shape
    halo_per_tile = tm // POOL_HALO
    row = pl.BlockSpec((tm, D), lambda i: (i, 0))
    return pl.pallas_call(
        functools.partial(_pool_router_kernel, blocks_per_seq=S // tm, n_experts=n_experts),
        grid=(T // tm,),
        in_specs=[
            row,
            pl.BlockSpec((POOL_HALO, D), lambda i: (jnp.maximum(i * halo_per_tile - 1, 0), 0)),
            _const_spec((1, D)), _const_spec((G, gw, gw)), _const_spec((1, D)), _const_spec((1, D)),
            _const_spec((1, D)), _const_spec((2, D, LANES)),
        ],
        out_specs=[row, row,
                   pl.BlockSpec((tm, LANES), lambda i: (i, 0)),
                   pl.BlockSpec((8, LANES), lambda i: (0, 0))],
        out_shape=[jax.ShapeDtypeStruct((T, D), F32), jax.ShapeDtypeStruct((T, D), F32),
                   jax.ShapeDtypeStruct((T, LANES), F32), jax.ShapeDtypeStruct((8, LANES), F32)],
        scratch_shapes=[pltpu.VMEM((tm + POOL_HALO, D), F32), pltpu.VMEM((8, LANES), F32),
                        pltpu.VMEM((tm + POOL_HALO, D), F32), pltpu.VMEM((tm + POOL_HALO, D), F32)],
        compiler_params=_params("arbitrary"),
        name="pool_router",
    )(h, h, g_mix, pool_w, pool_b, pool_scale, g_ffn, w_router_pad)


def _scatter_kernel(pos_ref, pad_ref, blk_ref, inv_init_ref, x_ref, out_ref, inv_ref, sem):
    n = x_ref.shape[0]
    n_tokens = n * pl.num_programs(0)
    first = pl.program_id(0) * n

    def copy(src_row, n_rows, dst_row):
        return pltpu.make_async_copy(x_ref.at[pl.ds(src_row, n_rows)], out_ref.at[pl.ds(dst_row, n_rows)], sem)

    def issue(group, carry):
        r0 = pl.multiple_of(group * SUBLANES, SUBLANES)
        for q in range(SUBLANES):
            for k in range(TOP_K):
                p = pos_ref[0, k, r0 + q]
                copy(r0 + q, 1, p).start()
                inv_ref[EXPERT_TILE + p] = k * n_tokens + first + r0 + q
        return carry

    lax.fori_loop(0, n // SUBLANES, issue, 0)

    @pl.when(pl.program_id(0) == pl.num_programs(0) - 1)
    def _():
        def keep_lead(r, carry):
            inv_ref[r] = inv_init_ref[r]
            return carry

        lax.fori_loop(0, EXPERT_TILE, keep_lead, 0, unroll=8)

        for e in range(pad_ref.shape[1]):
            start = pad_ref[0, e]
            count = pad_ref[1, e]

            def fill(r, carry):
                copy(0, 1, start + r).start()
                inv_ref[EXPERT_TILE + start + r] = inv_init_ref[EXPERT_TILE + start + r]
                return carry

            lax.fori_loop(0, count, fill, 0)

            whole = pl.multiple_of(count // 8 * 8, 8)

            @pl.when(whole > 0)
            def _():
                copy(0, whole, 0).wait()

            def drain(r, carry):
                copy(0, 1, 0).wait()
                return carry

            lax.fori_loop(0, count - whole, drain, 0)

        for k in range(blk_ref.shape[0]):
            blk = blk_ref[k]

            @pl.when(blk >= 0)
            def _():
                c = copy(0, EXPERT_SUB, pl.multiple_of(blk * EXPERT_SUB, EXPERT_SUB))
                c.start()

                def keep(r, carry):
                    inv_ref[EXPERT_TILE + blk * EXPERT_SUB + r] = inv_init_ref[EXPERT_TILE + blk * EXPERT_SUB + r]
                    return carry

                lax.fori_loop(0, EXPERT_SUB, keep, 0, unroll=8)
                c.wait()

    for k in range(TOP_K):
        copy(0, n, 0).wait()


def _row_scatter(x, pos, pad, empty_blocks, n_out):
    T, D = x.shape
    n = GATHER_ROWS
    assert EXPERT_SUB <= n
    pos_blocks = pos.reshape(T // n, n, TOP_K).transpose(0, 2, 1)
    pad_dst = TOP_K * T + jnp.arange(n_out + EXPERT_TILE, dtype=jnp.int32) % EXPERT_TILE
    return pl.pallas_call(
        _scatter_kernel,
        grid=(T // n,),
        in_specs=[pl.BlockSpec((1, TOP_K, n), lambda i: (i, 0, 0), memory_space=pltpu.SMEM),
                  pl.BlockSpec(memory_space=pltpu.SMEM),
                  pl.BlockSpec(memory_space=pltpu.SMEM),
                  pl.BlockSpec(memory_space=pltpu.SMEM),
                  pl.BlockSpec((n, D), lambda i: (i, 0))],
        out_specs=[pl.BlockSpec(memory_space=pl.ANY), pl.BlockSpec(memory_space=pltpu.SMEM)],
        out_shape=[jax.ShapeDtypeStruct((n_out, D), x.dtype),
                   jax.ShapeDtypeStruct((n_out + EXPERT_TILE,), jnp.int32)],
        scratch_shapes=[pltpu.SemaphoreType.DMA],
        compiler_params=_params("arbitrary"),
        name="row_scatter",
    )(pos_blocks, pad, empty_blocks, pad_dst, x)


def _expert_kernel(te_ref, tr_ref, tx_ref, inv_ref, x_ref, wg_ref, wu_ref, wd_ref, wg2_ref, wu2_ref, wd2_ref,
                   yk_ref, yo_ref, sem, *, n_f, n_blocks):
    i = pl.program_id(0)
    j = pl.program_id(1)
    tm, D = x_ref.shape
    sub = EXPERT_SUB
    chunk = tm // n_f // SUBLANES * SUBLANES
    rest = tm - n_f * chunk
    slot = i % 2
    prev = 1 - slot
    n_live = (tr_ref[i] + sub - 1) // sub

    def row_copy(buf, row, dst):
        return pltpu.make_async_copy(yo_ref.at[buf, pl.ds(row, 1)], yk_ref.at[pl.ds(dst, 1)], sem.at[buf])

    def wait_rows(buf):
        pltpu.make_async_copy(yo_ref.at[buf], yk_ref.at[pl.ds(0, tm)], sem.at[buf]).wait()

    def send_prev_rows(first_row, count):
        first_row = pl.multiple_of(first_row, SUBLANES)
        table = i * tm + first_row
        for q in range(count):
            row_copy(prev, first_row + q, inv_ref[table + q]).start()

    @pl.when((i == 0) & (j == 0))
    def _():
        yo_ref[...] = jnp.zeros(yo_ref.shape, F32)

    def sends_rows(t):
        return (t == 0) | (tr_ref[jnp.maximum(t - 1, 0)] > 0)

    @pl.when((i > 0) & (j == 0) & sends_rows(i - 1))
    def _():
        wait_rows(slot)

    for n in range(1, tm // sub + 1):
        m = n * sub

        @pl.when(n_live == n)
        def _():
            send_prev_rows(j * chunk, chunk)
            x = x_ref[:m, :].astype(BF16)
            part = _swiglu_chunks(x, wg_ref.at[0], wu_ref.at[0], wd_ref.at[0], MXU_COLS)
            yo_ref[slot, :m, :] = jnp.where(j > 0, yo_ref[slot, :m, :], 0.0) + part

            @pl.when(j == 0)
            def _():
                send_prev_rows(n_f * chunk, rest)
                if m < tm:
                    yo_ref[slot, m:, :] = jnp.zeros((tm - m, D), F32)

            @pl.when(2 * j + 1 < n_blocks)
            def _():
                x2 = x_ref[:m, :].astype(BF16)
                yo_ref[slot, :m, :] += _swiglu_chunks(x2, wg2_ref.at[0], wu2_ref.at[0], wd2_ref.at[0], MXU_COLS)

    @pl.when((n_live == 0) & sends_rows(i))
    def _():
        send_prev_rows(j * chunk, chunk)

        @pl.when(j == 0)
        def _():
            send_prev_rows(n_f * chunk, rest)

    @pl.when((i == pl.num_programs(0) - 1) & (j == n_f - 1) & sends_rows(i))
    def _():
        wait_rows(prev)


def _experts(xs, inv, tile_expert, tile_rows, tile_src, w_gate, w_up, w_down, n_tokens):
    P = xs.shape[0]
    E, D, F = w_gate.shape
    tm, tf = EXPERT_TILE, EXPERT_F_TILE
    n_blocks = F // tf
    n_f = -(-n_blocks // 2)
    dump_base = TOP_K * n_tokens
    dump_rows = tm

    def row_map(i, j, te, tr, tx, inv):
        return (tx[i], 0)

    def first_block(i, j, tr):
        return jnp.where(tr[i] > 0, 2 * j, 2 * (n_f - 1))

    def second_block(i, j, tr):
        blk = jnp.where(tr[i] > 0, 2 * j + 1, 2 * n_f - 1)
        return jnp.where(blk < n_blocks, blk, blk - 2)

    def cols(block):
        return lambda i, j, te, tr, tx, inv: (te[i], 0, block(i, j, tr))

    def rows(block):
        return lambda i, j, te, tr, tx, inv: (te[i], block(i, j, tr), 0)

    grid_spec = pltpu.PrefetchScalarGridSpec(
        num_scalar_prefetch=4,
        grid=(P // tm + 1, n_f),
        in_specs=[pl.BlockSpec((tm, D), row_map),
                  pl.BlockSpec((1, D, tf), cols(first_block)),
                  pl.BlockSpec((1, D, tf), cols(first_block)),
                  pl.BlockSpec((1, tf, D), rows(first_block)),
                  pl.BlockSpec((1, D, tf), cols(second_block)),
                  pl.BlockSpec((1, D, tf), cols(second_block)),
                  pl.BlockSpec((1, tf, D), rows(second_block))],
        out_specs=pl.BlockSpec(memory_space=pl.ANY),
        scratch_shapes=[pltpu.VMEM((2, tm, D), F32), pltpu.SemaphoreType.DMA((2,))],
    )
    return pl.pallas_call(
        functools.partial(_expert_kernel, n_f=n_f, n_blocks=n_blocks),
        grid_spec=grid_spec,
        out_shape=jax.ShapeDtypeStruct((dump_base + dump_rows, D), F32),
        compiler_params=_params("arbitrary", "arbitrary"),
        name="expert_swiglu",
    )(tile_expert, tile_rows, tile_src, inv, xs, w_gate, w_up, w_down, w_gate, w_up, w_down)


def _combine_kernel(h_ref, y1_ref, y2_ref, info_ref, p_ref, gp_ref, wpg_ref, wpp_ref, o_ref):
    info = info_ref[...]
    h = h_ref[...] + info[:, 2:3] * y1_ref[...] + info[:, 3:4] * y2_ref[...]
    o_ref[...] = _ple(h, p_ref[...], gp_ref[...], wpg_ref[...], wpp_ref[...])


def _combine(h, yk, info, p3, layer, g_ple, w_pg, w_pp):
    T, D = h.shape
    PD = p3.shape[2]
    tm = COMBINE_TILE
    nb = T // tm
    row = lambda w: pl.BlockSpec((tm, w), lambda i: (i, 0))
    return pl.pallas_call(
        _combine_kernel,
        grid=(nb,),
        in_specs=[row(D), row(D), pl.BlockSpec((tm, D), lambda i: (nb + i, 0)), row(LANES),
                  pl.BlockSpec((None, tm, PD), lambda i: (layer, i, 0)),
                  _const_spec((1, D)), _const_spec((D, D)), _const_spec((PD, D))],
        out_specs=row(D),
        out_shape=jax.ShapeDtypeStruct((T, D), F32),
        compiler_params=_params("parallel"),
        name="combine_ple",
    )(h, yk, yk, info, p3, g_ple, w_pg, w_pp)


def _rope_tables(S):
    inv = 1.0 / (ROPE_THETA ** (jnp.arange(0, DH, 2, dtype=F32) / DH))
    ang = jnp.arange(S, dtype=F32)[:, None] * inv[None, :]
    cos, sin = jnp.cos(ang), jnp.sin(ang)
    reps = HEAD_W // DH
    cos_t = jnp.tile(jnp.concatenate([cos, cos], axis=1), (1, reps))
    sin_t = jnp.tile(jnp.concatenate([-sin, sin], axis=1), (1, reps))
    return cos_t, sin_t


def _routing_tables(info, counts_f, n_experts, T):
    tm = EXPERT_TILE
    e12 = info[:, 0:2].astype(jnp.int32)
    rank = info[:, 4:6].astype(jnp.int32)
    counts = counts_f[0, :n_experts].astype(jnp.int32)
    padded = (counts + tm - 1) // tm * tm
    ends = jnp.cumsum(padded)
    starts = ends - padded
    onehot = e12[:, :, None] == jnp.arange(n_experts, dtype=jnp.int32)
    pos = jnp.sum(jnp.where(onehot, starts, 0), axis=-1) + rank
    P = T * TOP_K + n_experts * tm
    sub = EXPERT_SUB
    pad = jnp.stack([starts + counts, (counts + sub - 1) // sub * sub - counts])
    blk_start = jnp.arange(P // sub, dtype=jnp.int32) * sub
    holds_rows = jnp.any((blk_start[:, None] >= starts[None, :]) & (blk_start[:, None] < (starts + counts)[None, :]), axis=1)
    slot = jnp.where(holds_rows, -1, jnp.cumsum(~holds_rows) - 1)
    hit = slot[None, :] == jnp.arange(n_experts * tm // sub, dtype=jnp.int32)[:, None]
    empty_blocks = (jnp.sum(jnp.where(hit, jnp.arange(P // sub, dtype=jnp.int32) + 1, 0), axis=1) - 1).astype(jnp.int32)
    n_tiles = P // tm + 1
    tile_start = jnp.arange(n_tiles, dtype=jnp.int32) * tm
    tile_expert = jnp.minimum(jnp.sum(tile_start[:, None] >= ends[None, :], axis=1), n_experts - 1).astype(jnp.int32)
    tile_rows = jnp.clip((starts + counts)[tile_expert] - tile_start, 0, tm).astype(jnp.int32)
    tile_src = jnp.minimum(jnp.arange(n_tiles, dtype=jnp.int32), ends[-1] // tm - 1).astype(jnp.int32)
    return pos, pad, empty_blocks, P, tile_expert, tile_rows, tile_src


def kernel(x, p, norm_mix, norm_ffn, norm_ple, attn_w_qkv, attn_w_o, attn_q_norm, attn_k_norm,
           attn_lambda_q1, attn_lambda_k1, attn_lambda_q2, attn_lambda_k2, attn_subln,
           pool_w, pool_b, pool_scale, ffn_w_gate, ffn_w_up, ffn_w_down,
           moe_router, moe_w_gate, moe_w_up, moe_w_down, ple_w_proj, ple_w_gate):
    B, S, D = x.shape
    T = B * S
    n_experts = moe_router.shape[-1]
    x2 = x.reshape(T, D)
    p3 = p.reshape(p.shape[0], T, p.shape[-1])
    bf = lambda a: a.astype(BF16)
    vec = lambda a: a.reshape(1, -1)

    lam_init = 0.8 - 0.6 * math.exp(-0.3 * 0)
    cos_t, sin_t = _rope_tables(S)
    reps = HEAD_W // DH
    qk_gain = jnp.zeros((8, LANES), F32)
    qk_gain = qk_gain.at[0].set(jnp.tile(attn_q_norm[0], reps) * (ATTN_SCALE * math.log2(math.e)))
    qk_gain = qk_gain.at[1].set(jnp.tile(attn_k_norm[0], reps))
    group = jnp.arange(2 * LANES) // DH
    ones = (group[:, None] == group[None, :]).astype(BF16)
    qkv = _qkv(x2, vec(norm_mix[0]), attn_w_qkv[0], cos_t, sin_t, qk_gain, ones)
    lam_params = jnp.stack([attn_lambda_q1[0], attn_lambda_k1[0], attn_lambda_q2[0], attn_lambda_k2[0]])
    ao = _attention(qkv.reshape(B, S, 3 * D), lam_params, vec(attn_subln[0]), lam_init)
    h = _dense_tail(ao.reshape(T, D), x2, p3, 0, bf(attn_w_o[0]), vec(norm_ffn[0]),
                    bf(ffn_w_gate[0]), bf(ffn_w_up[0]), bf(ffn_w_down[0]),
                    vec(norm_ple[0]), bf(ple_w_gate[0]), bf(ple_w_proj[0]))

    w_router = jnp.zeros((D, LANES), F32).at[:, :n_experts].set(moe_router[0])
    w_router_hi = w_router.astype(BF16)
    w_router_pad = jnp.stack([w_router_hi, (w_router - w_router_hi.astype(F32)).astype(BF16)])
    h, xn, info, counts = _pool_router(h, S, vec(norm_mix[1]), bf(pool_w[0]), vec(pool_b[0]),
                                       vec(pool_scale[0]), vec(norm_ffn[1]), w_router_pad, n_experts)
    pos, pad, empty_blocks, n_sorted, tile_expert, tile_rows, tile_src = _routing_tables(info, counts, n_experts, T)
    xs, inv = _row_scatter(xn, pos, pad, empty_blocks, n_sorted)
    yk = _experts(xs, inv, tile_expert, tile_rows, tile_src, moe_w_gate[0], moe_w_up[0], moe_w_down[0], T)
    out = _combine(h, yk, info, p3, 1, vec(norm_ple[1]), bf(ple_w_gate[1]), bf(ple_w_proj[1]))
    return out.reshape(B, S, D)
```

```python
import functools
import math

import jax
import jax.numpy as jnp
from jax import lax
from jax.experimental import pallas as pl
from jax.experimental.pallas import tpu as pltpu

F32 = jnp.float32
BF16 = jnp.bfloat16

DH = 64
HEAD_W = 2 * DH
ATTN_SCALE = 1.0 / math.sqrt(DH)
ROPE_THETA = 10000.0
POOL_WINDOWS = (2, 4, 8, 16)
POOL_HALO = 32
TOP_K = 2
RMS_EPS = 1e-6
LANES = 128
SUBLANES = 8
MXU_COLS = 256
DENSE_F_CHUNK = 512
VMEM_LIMIT = 56 * 1024 * 1024

ROW_TILE = 512
COMBINE_TILE = 1024
Q_TILE = 256
ATTN_HEADS_PER_STEP = 2
EXPERT_TILE = 1024
EXPERT_SUB = 256
EXPERT_F_TILE = 512
GATHER_ROWS = 2048


def _rms(x, gain):
    return x * lax.rsqrt(jnp.mean(x * x, axis=-1, keepdims=True) + RMS_EPS) * gain


def _const_spec(shape):
    zeros = (0,) * len(shape)
    return pl.BlockSpec(shape, lambda *_: zeros, pipeline_mode=pl.Buffered(1))


def _params(*sem):
    return pltpu.CompilerParams(dimension_semantics=sem, vmem_limit_bytes=VMEM_LIMIT)


def _qkv_kernel(x_ref, g_ref, w_ref, cos_ref, sin_ref, qkg_ref, ones_ref, o_ref, wb_ref):
    D = x_ref.shape[1]

    @pl.when(pl.program_id(0) == 0)
    def _():
        wb_ref[...] = w_ref[...].astype(BF16)

    xn = _rms(x_ref[...], g_ref[...]).astype(BF16)
    cos = cos_ref[...]
    sin = sin_ref[...]
    lane = lax.broadcasted_iota(jnp.int32, cos.shape, 1)
    first_half = (lane % DH) < (DH // 2)
    ones = ones_ref[...]
    width = ones.shape[0]
    def project(col):
        return jnp.dot(xn, wb_ref[:, col:col + width], preferred_element_type=F32)

    def finish(col, yc):
        part = col // D
        if part == 2:
            o_ref[:, col:col + width] = yc.astype(BF16)
            return
        gain = qkg_ref[part:part + 1, :]
        ss = jnp.dot((yc * yc).astype(BF16), ones, preferred_element_type=F32)
        yn = yc * lax.rsqrt(ss * (1.0 / DH) + RMS_EPS)
        for s in range(0, width, LANES):
            z = yn[:, s:s + LANES] * gain
            rot = jnp.where(first_half,
                            pltpu.roll(z, LANES - DH // 2, 1),
                            pltpu.roll(z, DH // 2, 1))
            o_ref[:, col + s:col + s + LANES] = (z * cos + rot * sin).astype(BF16)

    cols = list(range(0, 3 * D, width))
    pending = project(cols[0])
    for n, col in enumerate(cols):
        upcoming = project(cols[n + 1]) if n + 1 < len(cols) else None
        finish(col, pending)
        pending = upcoming


def _qkv(x2, gain, w_qkv, cos_t, sin_t, qk_gain, ones):
    T, D = x2.shape
    S = cos_t.shape[0]
    tm = ROW_TILE
    blocks_per_seq = S // tm
    return pl.pallas_call(
        _qkv_kernel,
        grid=(T // tm,),
        in_specs=[
            pl.BlockSpec((tm, D), lambda i: (i, 0)),
            _const_spec((1, D)),
            _const_spec((D, 3 * D)),
            pl.BlockSpec((tm, LANES), lambda i: (i % blocks_per_seq, 0)),
            pl.BlockSpec((tm, LANES), lambda i: (i % blocks_per_seq, 0)),
            _const_spec(qk_gain.shape),
            _const_spec(ones.shape),
        ],
        out_specs=pl.BlockSpec((tm, 3 * D), lambda i: (i, 0)),
        out_shape=jax.ShapeDtypeStruct((T, 3 * D), BF16),
        scratch_shapes=[pltpu.VMEM((D, 3 * D), BF16)],
        compiler_params=_params("arbitrary"),
        name="qkv_rope",
    )(x2, gain, w_qkv, cos_t, sin_t, qk_gain, ones)


def _dot_nt(a, b):
    return lax.dot_general(a, b, (((1,), (1,)), ((), ())), preferred_element_type=F32)


def _attn_kernel(q_ref, k_ref, v_ref, lam_ref, g_ref, o_ref, vext_ref, *, lam_init):
    lp = lam_ref[...]
    lam = (jnp.exp(jnp.sum(lp[0:1] * lp[1:2], axis=-1, keepdims=True))
           - jnp.exp(jnp.sum(lp[2:3] * lp[3:4], axis=-1, keepdims=True)) + lam_init)
    S = q_ref.shape[1]
    tq = Q_TILE
    vlane = lax.broadcasted_iota(jnp.int32, (S, HEAD_W), 1)
    lane = lax.broadcasted_iota(jnp.int32, (tq, HEAD_W), 1)
    row = lax.broadcasted_iota(jnp.int32, (tq, tq), 0)
    col = lax.broadcasted_iota(jnp.int32, (tq, tq), 1)
    future = col > row
    gain = g_ref[...]
    n_heads = q_ref.shape[2] // HEAD_W
    for head in range(n_heads):
        vext_ref[head, :, :HEAD_W] = v_ref[0, :, head * HEAD_W:(head + 1) * HEAD_W]
        vext_ref[head, :, HEAD_W:] = jnp.where(vlane == 0, 1.0, 0.0).astype(BF16)

    def scores(head, i):
        cols = slice(head * HEAD_W, (head + 1) * HEAD_W)
        q = q_ref[0, i * tq:(i + 1) * tq, cols]
        out = []
        for qc in (jnp.where(lane < DH, q, jnp.zeros_like(q)), jnp.where(lane >= DH, q, jnp.zeros_like(q))):
            s_diag = jnp.where(future, -jnp.inf, _dot_nt(qc, k_ref[0, i * tq:(i + 1) * tq, cols]))
            s_off = _dot_nt(qc, k_ref[0, :i * tq, cols]) if i > 0 else None
            out.append((s_off, s_diag))
        return out

    def finish(head, i, maps):
        kv = (i + 1) * tq
        acc = []
        for s_off, s_diag in maps:
            m = jnp.max(s_diag, axis=-1, keepdims=True)
            if s_off is not None:
                m = jnp.maximum(m, jnp.max(s_off, axis=-1, keepdims=True))
                p = jnp.concatenate([jnp.exp2(s_off - m).astype(BF16),
                                     jnp.exp2(s_diag - m).astype(BF16)], axis=1)
            else:
                p = jnp.exp2(s_diag - m).astype(BF16)
            acc.append(jnp.dot(p, vext_ref[head, :kv, :], preferred_element_type=F32))
        o = (acc[0][:, :HEAD_W] * (1.0 / acc[0][:, HEAD_W:HEAD_W + 1])
             - acc[1][:, :HEAD_W] * (lam / acc[1][:, HEAD_W:HEAD_W + 1]))
        o = _rms(o, gain) * (1.0 - lam_init)
        o_ref[0, i * tq:(i + 1) * tq, head * HEAD_W:(head + 1) * HEAD_W] = o.astype(BF16)

    work = [(head, i) for head in range(n_heads) for i in range(S // tq)]
    pending = scores(*work[0])
    for pos, item in enumerate(work):
        upcoming = scores(*work[pos + 1]) if pos + 1 < len(work) else None
        finish(*item, pending)
        pending = upcoming


def _attention(qkv3, lam_params, subln, lam_init):
    B, S, W3 = qkv3.shape
    heads = ATTN_HEADS_PER_STEP
    H = W3 // 3 // (heads * HEAD_W)
    blk = (1, S, heads * HEAD_W)
    return pl.pallas_call(
        functools.partial(_attn_kernel, lam_init=lam_init),
        grid=(B, H),
        in_specs=[
            pl.BlockSpec(blk, lambda b, h: (b, 0, h)),
            pl.BlockSpec(blk, lambda b, h: (b, 0, H + h)),
            pl.BlockSpec(blk, lambda b, h: (b, 0, 2 * H + h)),
            _const_spec(lam_params.shape),
            _const_spec(subln.shape),
        ],
        out_specs=pl.BlockSpec(blk, lambda b, h: (b, 0, h)),
        out_shape=jax.ShapeDtypeStruct((B, S, W3 // 3), BF16),
        scratch_shapes=[pltpu.VMEM((heads, S, 2 * HEAD_W), BF16)],
        compiler_params=_params("parallel", "parallel"),
        name="diff_attn",
    )(qkv3, qkv3, qkv3, lam_params, subln)


def _ple(h, p, g_ple, w_gate, w_proj):
    xn = _rms(h, g_ple).astype(BF16)
    gate = jax.nn.sigmoid(jnp.dot(xn, w_gate, preferred_element_type=F32))
    proj = jnp.dot(p.astype(BF16), w_proj, preferred_element_type=F32)
    return h + gate * proj


def _swiglu_chunks(xn, wg_ref, wu_ref, wd_ref, width):
    F = wg_ref.shape[1]
    chunks = [(lo, min(lo + width, F)) for lo in range(0, F, width)]

    def gate_up(lo, hi):
        return (jnp.dot(xn, wg_ref[:, lo:hi].astype(BF16), preferred_element_type=F32),
                jnp.dot(xn, wu_ref[:, lo:hi].astype(BF16), preferred_element_type=F32))

    out = None
    pending = gate_up(*chunks[0])
    for c, (lo, hi) in enumerate(chunks):
        upcoming = gate_up(*chunks[c + 1]) if c + 1 < len(chunks) else None
        g, u = pending
        act = (g * jax.nn.sigmoid(g) * u).astype(BF16)
        part = jnp.dot(act, wd_ref[lo:hi, :].astype(BF16), preferred_element_type=F32)
        out = part if out is None else out + part
        pending = upcoming
    return out


def _dense_tail_kernel(ao_ref, x_ref, p_ref, wo_ref, gf_ref, wg_ref, wu_ref, wd_ref,
                       gp_ref, wpg_ref, wpp_ref, o_ref, *, f_chunk):
    h = x_ref[...] + jnp.dot(ao_ref[...], wo_ref[...], preferred_element_type=F32)
    xn = _rms(h, gf_ref[...]).astype(BF16)
    acc = h + _swiglu_chunks(xn, wg_ref, wu_ref, wd_ref, f_chunk)
    o_ref[...] = _ple(acc, p_ref[...], gp_ref[...], wpg_ref[...], wpp_ref[...])


def _dense_tail(ao, x2, p3, layer, w_o, g_ffn, w_gate, w_up, w_down, g_ple, w_pg, w_pp):
    T, D = x2.shape
    F = w_gate.shape[1]
    PD = p3.shape[2]
    tm = ROW_TILE
    row = lambda w: pl.BlockSpec((tm, w), lambda i: (i, 0))
    return pl.pallas_call(
        functools.partial(_dense_tail_kernel, f_chunk=DENSE_F_CHUNK),
        grid=(T // tm,),
        in_specs=[row(D), row(D), pl.BlockSpec((None, tm, PD), lambda i: (layer, i, 0)),
                  _const_spec((D, D)), _const_spec((1, D)),
                  _const_spec((D, F)), _const_spec((D, F)), _const_spec((F, D)),
                  _const_spec((1, D)), _const_spec((D, D)), _const_spec((PD, D))],
        out_specs=row(D),
        out_shape=jax.ShapeDtypeStruct((T, D), F32),
        compiler_params=_params("parallel"),
        name="dense_tail",
    )(ao, x2, p3, w_o, g_ffn, w_gate, w_up, w_down, g_ple, w_pg, w_pp)


def _pool_router_kernel(h_ref, halo_ref, gm_ref, pw_ref, pb_ref, ps_ref, gf_ref, wr_ref,
                        h_out_ref, xn_out_ref, info_ref, cnt_ref, xbuf_ref, carry_ref, *stage_refs,
                        blocks_per_seq, n_experts):
    i = pl.program_id(0)
    tm, D = h_ref.shape
    gw = D // len(POOL_WINDOWS)
    seq_block = i % blocks_per_seq

    @pl.when(i == 0)
    def _():
        carry_ref[...] = jnp.zeros_like(carry_ref)

    h = h_ref[...]
    gm = gm_ref[...]
    xbuf_ref[POOL_HALO:, :] = _rms(h, gm)

    @pl.when(seq_block == 0)
    def _():
        xbuf_ref[:POOL_HALO, :] = jnp.zeros((POOL_HALO, D), F32)

    @pl.when(seq_block != 0)
    def _():
        xbuf_ref[:POOL_HALO, :] = _rms(halo_ref[...], gm)

    n_rows = tm + POOL_HALO
    src, lo = xbuf_ref, 0
    windows = []
    for g, w in enumerate(POOL_WINDOWS):
        shift = w // 2
        assert w == 2 << g
        lo = -(-(lo + shift) // SUBLANES) * SUBLANES
        assert lo <= POOL_HALO
        val = src[lo:n_rows, g * gw:] + src[lo - shift:n_rows - shift, g * gw:]
        if g + 1 < len(POOL_WINDOWS):
            dst = stage_refs[g % 2]
            dst[lo:n_rows, g * gw:] = val
            windows.append(dst[POOL_HALO:n_rows, g * gw:(g + 1) * gw])
            src = dst
        else:
            windows.append(val[POOL_HALO - lo:, :])

    t = seq_block * tm + lax.broadcasted_iota(jnp.int32, (tm, 1), 0)
    pieces = []
    for g, w in enumerate(POOL_WINDOWS):
        tok = xbuf_ref[POOL_HALO:, g * gw:(g + 1) * gw]
        cnt = jnp.minimum(t + 1, w).astype(F32)
        y = (windows[g] / cnt - tok).astype(BF16)
        pieces.append(jnp.dot(y, pw_ref[g], preferred_element_type=F32))
    mix = (jnp.concatenate(pieces, axis=1) + pb_ref[...]) * ps_ref[...]
    h2 = h + mix
    h_out_ref[...] = h2
    xn = _rms(h2, gf_ref[...])
    xn_out_ref[...] = xn

    xh = xn.astype(BF16)
    xl = (xn - xh.astype(F32)).astype(BF16)
    logits = (jnp.dot(xh, wr_ref[0], preferred_element_type=F32)
              + jnp.dot(xl, wr_ref[0], preferred_element_type=F32)
              + jnp.dot(xh, wr_ref[1], preferred_element_type=F32))
    lane = lax.broadcasted_iota(jnp.int32, logits.shape, 1)
    lg = jnp.where(lane < n_experts, logits, -jnp.inf)
    m1 = jnp.max(lg, axis=-1, keepdims=True)
    i1 = jnp.min(jnp.where(lg == m1, lane, LANES), axis=-1, keepdims=True)
    lg2 = jnp.where(lane == i1, -jnp.inf, lg)
    m2 = jnp.max(lg2, axis=-1, keepdims=True)
    i2 = jnp.min(jnp.where(lg2 == m2, lane, LANES), axis=-1, keepdims=True)
    e = jnp.exp(m2 - m1)
    w1 = 1.0 / (1.0 + e)
    w2 = e / (1.0 + e)

    sel1 = lane == i1
    sel2 = lane == i2
    onehot = jnp.where(sel1 | sel2, 1.0, 0.0)
    r = lax.broadcasted_iota(jnp.int32, (tm, tm), 0)
    c = lax.broadcasted_iota(jnp.int32, (tm, tm), 1)
    strict_lower = jnp.where(c < r, 1.0, 0.0).astype(BF16)
    before = jnp.dot(strict_lower, onehot.astype(BF16), preferred_element_type=F32) + carry_ref[0:1, :]
    rank1 = jnp.sum(jnp.where(sel1, before, 0.0), axis=-1, keepdims=True)
    rank2 = jnp.sum(jnp.where(sel2, before, 0.0), axis=-1, keepdims=True)
    total = carry_ref[0:1, :] + jnp.sum(onehot, axis=0, keepdims=True)
    carry_ref[...] = jnp.broadcast_to(total, carry_ref.shape)
    cnt_ref[...] = jnp.broadcast_to(total, cnt_ref.shape)

    info = jnp.where(lane == 0, i1.astype(F32), 0.0)
    info = jnp.where(lane == 1, i2.astype(F32), info)
    info = jnp.where(lane == 2, w1, info)
    info = jnp.where(lane == 3, w2, info)
    info = jnp.where(lane == 4, rank1, info)
    info = jnp.where(lane == 5, rank2, info)
    info_ref[...] = info


def _pool_router(h, S, g_mix, pool_w, pool_b, pool_scale, g_ffn, w_router_pad, n_experts):
    T, D = h.shape
    tm = ROW_TILE
    G, gw, _ = pool_w.shape
    halo_per_tile = tm // POOL_HALO
    row = pl.BlockSpec((tm, D), lambda i: (i, 0))
    return pl.pallas_call(
        functools.partial(_pool_router_kernel, blocks_per_seq=S // tm, n_experts=n_experts),
        grid=(T // tm,),
        in_specs=[
            row,
            pl.BlockSpec((POOL_HALO, D), lambda i: (jnp.maximum(i * halo_per_tile - 1, 0), 0)),
            _const_spec((1, D)), _const_spec((G, gw, gw)), _const_spec((1, D)), _const_spec((1, D)),
            _const_spec((1, D)), _const_spec((2, D, LANES)),
        ],
        out_specs=[row, row,
                   pl.BlockSpec((tm, LANES), lambda i: (i, 0)),
                   pl.BlockSpec((8, LANES), lambda i: (0, 0))],
        out_shape=[jax.ShapeDtypeStruct((T, D), F32), jax.ShapeDtypeStruct((T, D), F32),
                   jax.ShapeDtypeStruct((T, LANES), F32), jax.ShapeDtypeStruct((8, LANES), F32)],
        scratch_shapes=[pltpu.VMEM((tm + POOL_HALO, D), F32), pltpu.VMEM((8, LANES), F32),
                        pltpu.VMEM((tm + POOL_HALO, D), F32), pltpu.VMEM((tm + POOL_HALO, D), F32)],
        compiler_params=_params("arbitrary"),
        name="pool_router",
    )(h, h, g_mix, pool_w, pool_b, pool_scale, g_ffn, w_router_pad)


def _scatter_kernel(pos_ref, pad_ref, blk_ref, inv_init_ref, x_ref, out_ref, inv_ref, sem):
    n = x_ref.shape[0]
    n_tokens = n * pl.num_programs(0)
    first = pl.program_id(0) * n

    def copy(src_row, n_rows, dst_row):
        return pltpu.make_async_copy(x_ref.at[pl.ds(src_row, n_rows)], out_ref.at[pl.ds(dst_row, n_rows)], sem)

    def issue(group, carry):
        r0 = pl.multiple_of(group * SUBLANES, SUBLANES)
        for q in range(SUBLANES):
            for k in range(TOP_K):
                p = pos_ref[0, k, r0 + q]
                copy(r0 + q, 1, p).start()
                inv_ref[EXPERT_TILE + p] = k * n_tokens + first + r0 + q
        return carry

    lax.fori_loop(0, n // SUBLANES, issue, 0)

    @pl.when(pl.program_id(0) == pl.num_programs(0) - 1)
    def _():
        def keep_lead(r, carry):
            inv_ref[r] = inv_init_ref[r]
            return carry

        lax.fori_loop(0, EXPERT_TILE, keep_lead, 0, unroll=8)

        for e in range(pad_ref.shape[1]):
            start = pad_ref[0, e]
            count = pad_ref[1, e]

            def fill(r, carry):
                copy(0, 1, start + r).start()
                inv_ref[EXPERT_TILE + start + r] = inv_init_ref[EXPERT_TILE + start + r]
                return carry

            lax.fori_loop(0, count, fill, 0)

            whole = pl.multiple_of(count // 8 * 8, 8)

            @pl.when(whole > 0)
            def _():
                copy(0, whole, 0).wait()

            def drain(r, carry):
                copy(0, 1, 0).wait()
                return carry

            lax.fori_loop(0, count - whole, drain, 0)

        for k in range(blk_ref.shape[0]):
            blk = blk_ref[k]

            @pl.when(blk >= 0)
            def _():
                c = copy(0, EXPERT_SUB, pl.multiple_of(blk * EXPERT_SUB, EXPERT_SUB))
                c.start()

                def keep(r, carry):
                    inv_ref[EXPERT_TILE + blk * EXPERT_SUB + r] = inv_init_ref[EXPERT_TILE + blk * EXPERT_SUB + r]
                    return carry

                lax.fori_loop(0, EXPERT_SUB, keep, 0, unroll=8)
                c.wait()

    for k in range(TOP_K):
        copy(0, n, 0).wait()


def _row_scatter(x, pos, pad, empty_blocks, n_out):
    T, D = x.shape
    n = GATHER_ROWS
    assert EXPERT_SUB <= n
    pos_blocks = pos.reshape(T // n, n, TOP_K).transpose(0, 2, 1)
    pad_dst = TOP_K * T + jnp.arange(n_out + EXPERT_TILE, dtype=jnp.int32) % EXPERT_TILE
    return pl.pallas_call(
        _scatter_kernel,
        grid=(T // n,),
        in_specs=[pl.BlockSpec((1, TOP_K, n), lambda i: (i, 0, 0), memory_space=pltpu.SMEM),
                  pl.BlockSpec(memory_space=pltpu.SMEM),
                  pl.BlockSpec(memory_space=pltpu.SMEM),
                  pl.BlockSpec(memory_space=pltpu.SMEM),
                  pl.BlockSpec((n, D), lambda i: (i, 0))],
        out_specs=[pl.BlockSpec(memory_space=pl.ANY), pl.BlockSpec(memory_space=pltpu.SMEM)],
        out_shape=[jax.ShapeDtypeStruct((n_out, D), x.dtype),
                   jax.ShapeDtypeStruct((n_out + EXPERT_TILE,), jnp.int32)],
        scratch_shapes=[pltpu.SemaphoreType.DMA],
        compiler_params=_params("arbitrary"),
        name="row_scatter",
    )(pos_blocks, pad, empty_blocks, pad_dst, x)


def _expert_kernel(te_ref, tr_ref, tx_ref, inv_ref, x_ref, wg_ref, wu_ref, wd_ref, yk_ref, yo_ref, sem,
                   *, n_f):
    i = pl.program_id(0)
    j = pl.program_id(1)
    tm, D = x_ref.shape
    sub = EXPERT_SUB
    chunk = tm // n_f // SUBLANES * SUBLANES
    rest = tm - n_f * chunk
    slot = i % 2
    prev = 1 - slot
    n_live = (tr_ref[i] + sub - 1) // sub

    def row_copy(buf, row, dst):
        return pltpu.make_async_copy(yo_ref.at[buf, pl.ds(row, 1)], yk_ref.at[pl.ds(dst, 1)], sem.at[buf])

    def wait_rows(buf):
        pltpu.make_async_copy(yo_ref.at[buf], yk_ref.at[pl.ds(0, tm)], sem.at[buf]).wait()

    def send_prev_rows(first_row, count):
        first_row = pl.multiple_of(first_row, SUBLANES)
        table = i * tm + first_row
        for q in range(count):
            row_copy(prev, first_row + q, inv_ref[table + q]).start()

    @pl.when((i == 0) & (j == 0))
    def _():
        yo_ref[...] = jnp.zeros(yo_ref.shape, F32)

    def sends_rows(t):
        return (t == 0) | (tr_ref[jnp.maximum(t - 1, 0)] > 0)

    @pl.when((i > 0) & (j == 0) & sends_rows(i - 1))
    def _():
        wait_rows(slot)

    for n in range(1, tm // sub + 1):
        m = n * sub

        @pl.when(n_live == n)
        def _():
            send_prev_rows(j * chunk, chunk)
            x = x_ref[:m, :].astype(BF16)
            part = _swiglu_chunks(x, wg_ref.at[0], wu_ref.at[0], wd_ref.at[0], MXU_COLS)
            yo_ref[slot, :m, :] = jnp.where(j > 0, yo_ref[slot, :m, :], 0.0) + part

            @pl.when(j == 0)
            def _():
                send_prev_rows(n_f * chunk, rest)
                if m < tm:
                    yo_ref[slot, m:, :] = jnp.zeros((tm - m, D), F32)

    @pl.when((n_live == 0) & sends_rows(i))
    def _():
        send_prev_rows(j * chunk, chunk)

        @pl.when(j == 0)
        def _():
            send_prev_rows(n_f * chunk, rest)

    @pl.when((i == pl.num_programs(0) - 1) & (j == n_f - 1) & sends_rows(i))
    def _():
        wait_rows(prev)


def _experts(xs, inv, tile_expert, tile_rows, tile_src, w_gate, w_up, w_down, n_tokens):
    P = xs.shape[0]
    E, D, F = w_gate.shape
    tm, tf = EXPERT_TILE, EXPERT_F_TILE
    n_f = F // tf
    dump_base = TOP_K * n_tokens
    dump_rows = tm

    def row_map(i, j, te, tr, tx, inv):
        return (tx[i], 0)

    def col_w(i, j, te, tr, tx, inv):
        return (te[i], 0, jnp.where(tr[i] > 0, j, n_f - 1))

    def row_w(i, j, te, tr, tx, inv):
        return (te[i], jnp.where(tr[i] > 0, j, n_f - 1), 0)

    grid_spec = pltpu.PrefetchScalarGridSpec(
        num_scalar_prefetch=4,
        grid=(P // tm + 1, n_f),
        in_specs=[pl.BlockSpec((tm, D), row_map),
                  pl.BlockSpec((1, D, tf), col_w),
                  pl.BlockSpec((1, D, tf), col_w),
                  pl.BlockSpec((1, tf, D), row_w)],
        out_specs=pl.BlockSpec(memory_space=pl.ANY),
        scratch_shapes=[pltpu.VMEM((2, tm, D), F32), pltpu.SemaphoreType.DMA((2,))],
    )
    return pl.pallas_call(
        functools.partial(_expert_kernel, n_f=n_f),
        grid_spec=grid_spec,
        out_shape=jax.ShapeDtypeStruct((dump_base + dump_rows, D), F32),
        compiler_params=_params("arbitrary", "arbitrary"),
        name="expert_swiglu",
    )(tile_expert, tile_rows, tile_src, inv, xs, w_gate, w_up, w_down)


def _combine_kernel(h_ref, y1_ref, y2_ref, info_ref, p_ref, gp_ref, wpg_ref, wpp_ref, o_ref):
    info = info_ref[...]
    h = h_ref[...] + info[:, 2:3] * y1_ref[...] + info[:, 3:4] * y2_ref[...]
    o_ref[...] = _ple(h, p_ref[...], gp_ref[...], wpg_ref[...], wpp_ref[...])


def _combine(h, yk, info, p3, layer, g_ple, w_pg, w_pp):
    T, D = h.shape
    PD = p3.shape[2]
    tm = COMBINE_TILE
    nb = T // tm
    row = lambda w: pl.BlockSpec((tm, w), lambda i: (i, 0))
    return pl.pallas_call(
        _combine_kernel,
        grid=(nb,),
        in_specs=[row(D), row(D), pl.BlockSpec((tm, D), lambda i: (nb + i, 0)), row(LANES),
                  pl.BlockSpec((None, tm, PD), lambda i: (layer, i, 0)),
                  _const_spec((1, D)), _const_spec((D, D)), _const_spec((PD, D))],
        out_specs=row(D),
        out_shape=jax.ShapeDtypeStruct((T, D), F32),
        compiler_params=_params("parallel"),
        name="combine_ple",
    )(h, yk, yk, info, p3, g_ple, w_pg, w_pp)


def _rope_tables(S):
    inv = 1.0 / (ROPE_THETA ** (jnp.arange(0, DH, 2, dtype=F32) / DH))
    ang = jnp.arange(S, dtype=F32)[:, None] * inv[None, :]
    cos, sin = jnp.cos(ang), jnp.sin(ang)
    reps = HEAD_W // DH
    cos_t = jnp.tile(jnp.concatenate([cos, cos], axis=1), (1, reps))
    sin_t = jnp.tile(jnp.concatenate([-sin, sin], axis=1), (1, reps))
    return cos_t, sin_t


def _routing_tables(info, counts_f, n_experts, T):
    tm = EXPERT_TILE
    e12 = info[:, 0:2].astype(jnp.int32)
    rank = info[:, 4:6].astype(jnp.int32)
    counts = counts_f[0, :n_experts].astype(jnp.int32)
    padded = (counts + tm - 1) // tm * tm
    ends = jnp.cumsum(padded)
    starts = ends - padded
    onehot = e12[:, :, None] == jnp.arange(n_experts, dtype=jnp.int32)
    pos = jnp.sum(jnp.where(onehot, starts, 0), axis=-1) + rank
    P = T * TOP_K + n_experts * tm
    sub = EXPERT_SUB
    pad = jnp.stack([starts + counts, (counts + sub - 1) // sub * sub - counts])
    blk_start = jnp.arange(P // sub, dtype=jnp.int32) * sub
    holds_rows = jnp.any((blk_start[:, None] >= starts[None, :]) & (blk_start[:, None] < (starts + counts)[None, :]), axis=1)
    slot = jnp.where(holds_rows, -1, jnp.cumsum(~holds_rows) - 1)
    hit = slot[None, :] == jnp.arange(n_experts * tm // sub, dtype=jnp.int32)[:, None]
    empty_blocks = (jnp.sum(jnp.where(hit, jnp.arange(P // sub, dtype=jnp.int32) + 1, 0), axis=1) - 1).astype(jnp.int32)
    n_tiles = P // tm + 1
    tile_start = jnp.arange(n_tiles, dtype=jnp.int32) * tm
    tile_expert = jnp.minimum(jnp.sum(tile_start[:, None] >= ends[None, :], axis=1), n_experts - 1).astype(jnp.int32)
    tile_rows = jnp.clip((starts + counts)[tile_expert] - tile_start, 0, tm).astype(jnp.int32)
    tile_src = jnp.minimum(jnp.arange(n_tiles, dtype=jnp.int32), ends[-1] // tm - 1).astype(jnp.int32)
    return pos, pad, empty_blocks, P, tile_expert, tile_rows, tile_src


def kernel(x, p, norm_mix, norm_ffn, norm_ple, attn_w_qkv, attn_w_o, attn_q_norm, attn_k_norm,
           attn_lambda_q1, attn_lambda_k1, attn_lambda_q2, attn_lambda_k2, attn_subln,
           pool_w, pool_b, pool_scale, ffn_w_gate, ffn_w_up, ffn_w_down,
           moe_router, moe_w_gate, moe_w_up, moe_w_down, ple_w_proj, ple_w_gate):
    B, S, D = x.shape
    T = B * S
    n_experts = moe_router.shape[-1]
    x2 = x.reshape(T, D)
    p3 = p.reshape(p.shape[0], T, p.shape[-1])
    bf = lambda a: a.astype(BF16)
    vec = lambda a: a.reshape(1, -1)

    lam_init = 0.8 - 0.6 * math.exp(-0.3 * 0)
    cos_t, sin_t = _rope_tables(S)
    reps = HEAD_W // DH
    qk_gain = jnp.zeros((8, LANES), F32)
    qk_gain = qk_gain.at[0].set(jnp.tile(attn_q_norm[0], reps) * (ATTN_SCALE * math.log2(math.e)))
    qk_gain = qk_gain.at[1].set(jnp.tile(attn_k_norm[0], reps))
    group = jnp.arange(2 * LANES) // DH
    ones = (group[:, None] == group[None, :]).astype(BF16)
    qkv = _qkv(x2, vec(norm_mix[0]), attn_w_qkv[0], cos_t, sin_t, qk_gain, ones)
    lam_params = jnp.stack([attn_lambda_q1[0], attn_lambda_k1[0], attn_lambda_q2[0], attn_lambda_k2[0]])
    ao = _attention(qkv.reshape(B, S, 3 * D), lam_params, vec(attn_subln[0]), lam_init)
    h = _dense_tail(ao.reshape(T, D), x2, p3, 0, bf(attn_w_o[0]), vec(norm_ffn[0]),
                    bf(ffn_w_gate[0]), bf(ffn_w_up[0]), bf(ffn_w_down[0]),
                    vec(norm_ple[0]), bf(ple_w_gate[0]), bf(ple_w_proj[0]))

    w_router = jnp.zeros((D, LANES), F32).at[:, :n_experts].set(moe_router[0])
    w_router_hi = w_router.astype(BF16)
    w_router_pad = jnp.stack([w_router_hi, (w_router - w_router_hi.astype(F32)).astype(BF16)])
    h, xn, info, counts = _pool_router(h, S, vec(norm_mix[1]), bf(pool_w[0]), vec(pool_b[0]),
                                       vec(pool_scale[0]), vec(norm_ffn[1]), w_router_pad, n_experts)
    pos, pad, empty_blocks, n_sorted, tile_expert, tile_rows, tile_src = _routing_tables(info, counts, n_experts, T)
    xs, inv = _row_scatter(xn, pos, pad, empty_blocks, n_sorted)
    yk = _experts(xs, inv, tile_expert, tile_rows, tile_src, moe_w_gate[0], moe_w_up[0], moe_w_down[0], T)
    out = _combine(h, yk, info, p3, 1, vec(norm_ple[1]), bf(ple_w_gate[1]), bf(ple_w_proj[1]))
    return out.reshape(B, S, D)
```

```python
import functools
import math

import jax
import jax.numpy as jnp
from jax import lax
from jax.experimental import pallas as pl
from jax.experimental.pallas import tpu as pltpu

F32 = jnp.float32
BF16 = jnp.bfloat16

DH = 64
HEAD_W = 2 * DH
ATTN_SCALE = 1.0 / math.sqrt(DH)
ROPE_THETA = 10000.0
POOL_WINDOWS = (2, 4, 8, 16)
POOL_HALO = 32
TOP_K = 2
RMS_EPS = 1e-6
LANES = 128
SUBLANES = 8
MXU_COLS = 256
DENSE_F_CHUNK = 512
VMEM_LIMIT = 56 * 1024 * 1024

ROW_TILE = 512
COMBINE_TILE = 1024
Q_TILE = 256
ATTN_HEADS_PER_STEP = 2
EXPERT_TILE = 1024
EXPERT_SUB = 256
EXPERT_F_TILE = 512
GATHER_ROWS = 2048


def _rms(x, gain):
    return x * lax.rsqrt(jnp.mean(x * x, axis=-1, keepdims=True) + RMS_EPS) * gain


def _const_spec(shape):
    zeros = (0,) * len(shape)
    return pl.BlockSpec(shape, lambda *_: zeros, pipeline_mode=pl.Buffered(1))


def _params(*sem):
    return pltpu.CompilerParams(dimension_semantics=sem, vmem_limit_bytes=VMEM_LIMIT)


def _qkv_kernel(x_ref, g_ref, w_ref, cos_ref, sin_ref, qkg_ref, ones_ref, o_ref, wb_ref):
    D = x_ref.shape[1]

    @pl.when(pl.program_id(0) == 0)
    def _():
        wb_ref[...] = w_ref[...].astype(BF16)

    xn = _rms(x_ref[...], g_ref[...]).astype(BF16)
    cos = cos_ref[...]
    sin = sin_ref[...]
    lane = lax.broadcasted_iota(jnp.int32, cos.shape, 1)
    first_half = (lane % DH) < (DH // 2)
    ones = ones_ref[...]
    width = ones.shape[0]
    def project(col):
        return jnp.dot(xn, wb_ref[:, col:col + width], preferred_element_type=F32)

    def finish(col, yc):
        part = col // D
        if part == 2:
            o_ref[:, col:col + width] = yc.astype(BF16)
            return
        gain = qkg_ref[part:part + 1, :]
        ss = jnp.dot((yc * yc).astype(BF16), ones, preferred_element_type=F32)
        yn = yc * lax.rsqrt(ss * (1.0 / DH) + RMS_EPS)
        for s in range(0, width, LANES):
            z = yn[:, s:s + LANES] * gain
            rot = jnp.where(first_half,
                            pltpu.roll(z, LANES - DH // 2, 1),
                            pltpu.roll(z, DH // 2, 1))
            o_ref[:, col + s:col + s + LANES] = (z * cos + rot * sin).astype(BF16)

    cols = list(range(0, 3 * D, width))
    pending = project(cols[0])
    for n, col in enumerate(cols):
        upcoming = project(cols[n + 1]) if n + 1 < len(cols) else None
        finish(col, pending)
        pending = upcoming


def _qkv(x2, gain, w_qkv, cos_t, sin_t, qk_gain, ones):
    T, D = x2.shape
    S = cos_t.shape[0]
    tm = ROW_TILE
    blocks_per_seq = S // tm
    return pl.pallas_call(
        _qkv_kernel,
        grid=(T // tm,),
        in_specs=[
            pl.BlockSpec((tm, D), lambda i: (i, 0)),
            _const_spec((1, D)),
            _const_spec((D, 3 * D)),
            pl.BlockSpec((tm, LANES), lambda i: (i % blocks_per_seq, 0)),
            pl.BlockSpec((tm, LANES), lambda i: (i % blocks_per_seq, 0)),
            _const_spec(qk_gain.shape),
            _const_spec(ones.shape),
        ],
        out_specs=pl.BlockSpec((tm, 3 * D), lambda i: (i, 0)),
        out_shape=jax.ShapeDtypeStruct((T, 3 * D), BF16),
        scratch_shapes=[pltpu.VMEM((D, 3 * D), BF16)],
        compiler_params=_params("arbitrary"),
        name="qkv_rope",
    )(x2, gain, w_qkv, cos_t, sin_t, qk_gain, ones)


def _dot_nt(a, b):
    return lax.dot_general(a, b, (((1,), (1,)), ((), ())), preferred_element_type=F32)


def _attn_kernel(q_ref, k_ref, v_ref, lam_ref, g_ref, o_ref, vext_ref, *, lam_init):
    lp = lam_ref[...]
    lam = (jnp.exp(jnp.sum(lp[0:1] * lp[1:2], axis=-1, keepdims=True))
           - jnp.exp(jnp.sum(lp[2:3] * lp[3:4], axis=-1, keepdims=True)) + lam_init)
    S = q_ref.shape[1]
    tq = Q_TILE
    vlane = lax.broadcasted_iota(jnp.int32, (S, HEAD_W), 1)
    lane = lax.broadcasted_iota(jnp.int32, (tq, HEAD_W), 1)
    row = lax.broadcasted_iota(jnp.int32, (tq, tq), 0)
    col = lax.broadcasted_iota(jnp.int32, (tq, tq), 1)
    future = col > row
    gain = g_ref[...]
    n_heads = q_ref.shape[2] // HEAD_W
    for head in range(n_heads):
        vext_ref[head, :, :HEAD_W] = v_ref[0, :, head * HEAD_W:(head + 1) * HEAD_W]
        vext_ref[head, :, HEAD_W:] = jnp.where(vlane == 0, 1.0, 0.0).astype(BF16)

    def scores(head, i):
        cols = slice(head * HEAD_W, (head + 1) * HEAD_W)
        q = q_ref[0, i * tq:(i + 1) * tq, cols]
        out = []
        for qc in (jnp.where(lane < DH, q, jnp.zeros_like(q)), jnp.where(lane >= DH, q, jnp.zeros_like(q))):
            s_diag = jnp.where(future, -jnp.inf, _dot_nt(qc, k_ref[0, i * tq:(i + 1) * tq, cols]))
            s_off = _dot_nt(qc, k_ref[0, :i * tq, cols]) if i > 0 else None
            out.append((s_off, s_diag))
        return out

    def finish(head, i, maps):
        kv = (i + 1) * tq
        acc = []
        for s_off, s_diag in maps:
            m = jnp.max(s_diag, axis=-1, keepdims=True)
            if s_off is not None:
                m = jnp.maximum(m, jnp.max(s_off, axis=-1, keepdims=True))
                p = jnp.concatenate([jnp.exp2(s_off - m).astype(BF16),
                                     jnp.exp2(s_diag - m).astype(BF16)], axis=1)
            else:
                p = jnp.exp2(s_diag - m).astype(BF16)
            acc.append(jnp.dot(p, vext_ref[head, :kv, :], preferred_element_type=F32))
        o = (acc[0][:, :HEAD_W] * (1.0 / acc[0][:, HEAD_W:HEAD_W + 1])
             - acc[1][:, :HEAD_W] * (lam / acc[1][:, HEAD_W:HEAD_W + 1]))
        o = _rms(o, gain) * (1.0 - lam_init)
        o_ref[0, i * tq:(i + 1) * tq, head * HEAD_W:(head + 1) * HEAD_W] = o.astype(BF16)

    work = [(head, i) for head in range(n_heads) for i in range(S // tq)]
    pending = scores(*work[0])
    for pos, item in enumerate(work):
        upcoming = scores(*work[pos + 1]) if pos + 1 < len(work) else None
        finish(*item, pending)
        pending = upcoming


def _attention(qkv3, lam_params, subln, lam_init):
    B, S, W3 = qkv3.shape
    heads = ATTN_HEADS_PER_STEP
    H = W3 // 3 // (heads * HEAD_W)
    blk = (1, S, heads * HEAD_W)
    return pl.pallas_call(
        functools.partial(_attn_kernel, lam_init=lam_init),
        grid=(B, H),
        in_specs=[
            pl.BlockSpec(blk, lambda b, h: (b, 0, h)),
            pl.BlockSpec(blk, lambda b, h: (b, 0, H + h)),
            pl.BlockSpec(blk, lambda b, h: (b, 0, 2 * H + h)),
            _const_spec(lam_params.shape),
            _const_spec(subln.shape),
        ],
        out_specs=pl.BlockSpec(blk, lambda b, h: (b, 0, h)),
        out_shape=jax.ShapeDtypeStruct((B, S, W3 // 3), BF16),
        scratch_shapes=[pltpu.VMEM((heads, S, 2 * HEAD_W), BF16)],
        compiler_params=_params("parallel", "parallel"),
        name="diff_attn",
    )(qkv3, qkv3, qkv3, lam_params, subln)


def _ple(h, p, g_ple, w_gate, w_proj):
    xn = _rms(h, g_ple).astype(BF16)
    gate = jax.nn.sigmoid(jnp.dot(xn, w_gate, preferred_element_type=F32))
    proj = jnp.dot(p.astype(BF16), w_proj, preferred_element_type=F32)
    return h + gate * proj


def _swiglu_chunks(xn, wg_ref, wu_ref, wd_ref, width):
    F = wg_ref.shape[1]
    chunks = [(lo, min(lo + width, F)) for lo in range(0, F, width)]

    def gate_up(lo, hi):
        return (jnp.dot(xn, wg_ref[:, lo:hi].astype(BF16), preferred_element_type=F32),
                jnp.dot(xn, wu_ref[:, lo:hi].astype(BF16), preferred_element_type=F32))

    out = None
    pending = gate_up(*chunks[0])
    for c, (lo, hi) in enumerate(chunks):
        upcoming = gate_up(*chunks[c + 1]) if c + 1 < len(chunks) else None
        g, u = pending
        act = (g * jax.nn.sigmoid(g) * u).astype(BF16)
        part = jnp.dot(act, wd_ref[lo:hi, :].astype(BF16), preferred_element_type=F32)
        out = part if out is None else out + part
        pending = upcoming
    return out


def _dense_tail_kernel(ao_ref, x_ref, p_ref, wo_ref, gf_ref, wg_ref, wu_ref, wd_ref,
                       gp_ref, wpg_ref, wpp_ref, o_ref, *, f_chunk):
    h = x_ref[...] + jnp.dot(ao_ref[...], wo_ref[...], preferred_element_type=F32)
    xn = _rms(h, gf_ref[...]).astype(BF16)
    acc = h + _swiglu_chunks(xn, wg_ref, wu_ref, wd_ref, f_chunk)
    o_ref[...] = _ple(acc, p_ref[...], gp_ref[...], wpg_ref[...], wpp_ref[...])


def _dense_tail(ao, x2, p3, layer, w_o, g_ffn, w_gate, w_up, w_down, g_ple, w_pg, w_pp):
    T, D = x2.shape
    F = w_gate.shape[1]
    PD = p3.shape[2]
    tm = ROW_TILE
    row = lambda w: pl.BlockSpec((tm, w), lambda i: (i, 0))
    return pl.pallas_call(
        functools.partial(_dense_tail_kernel, f_chunk=DENSE_F_CHUNK),
        grid=(T // tm,),
        in_specs=[row(D), row(D), pl.BlockSpec((None, tm, PD), lambda i: (layer, i, 0)),
                  _const_spec((D, D)), _const_spec((1, D)),
                  _const_spec((D, F)), _const_spec((D, F)), _const_spec((F, D)),
                  _const_spec((1, D)), _const_spec((D, D)), _const_spec((PD, D))],
        out_specs=row(D),
        out_shape=jax.ShapeDtypeStruct((T, D), F32),
        compiler_params=_params("parallel"),
        name="dense_tail",
    )(ao, x2, p3, w_o, g_ffn, w_gate, w_up, w_down, g_ple, w_pg, w_pp)


def _pool_router_kernel(h_ref, halo_ref, gm_ref, pw_ref, pb_ref, ps_ref, gf_ref, wr_ref,
                        h_out_ref, xn_out_ref, info_ref, cnt_ref, xbuf_ref, carry_ref, *stage_refs,
                        blocks_per_seq, n_experts):
    i = pl.program_id(0)
    tm, D = h_ref.shape
    gw = D // len(POOL_WINDOWS)
    seq_block = i % blocks_per_seq

    @pl.when(i == 0)
    def _():
        carry_ref[...] = jnp.zeros_like(carry_ref)

    h = h_ref[...]
    gm = gm_ref[...]
    xbuf_ref[POOL_HALO:, :] = _rms(h, gm)

    @pl.when(seq_block == 0)
    def _():
        xbuf_ref[:POOL_HALO, :] = jnp.zeros((POOL_HALO, D), F32)

    @pl.when(seq_block != 0)
    def _():
        xbuf_ref[:POOL_HALO, :] = _rms(halo_ref[...], gm)

    n_rows = tm + POOL_HALO
    src, lo = xbuf_ref, 0
    windows = []
    for g, w in enumerate(POOL_WINDOWS):
        shift = w // 2
        assert w == 2 << g
        lo = -(-(lo + shift) // SUBLANES) * SUBLANES
        assert lo <= POOL_HALO
        val = src[lo:n_rows, g * gw:] + src[lo - shift:n_rows - shift, g * gw:]
        if g + 1 < len(POOL_WINDOWS):
            dst = stage_refs[g % 2]
            dst[lo:n_rows, g * gw:] = val
            windows.append(dst[POOL_HALO:n_rows, g * gw:(g + 1) * gw])
            src = dst
        else:
            windows.append(val[POOL_HALO - lo:, :])

    t = seq_block * tm + lax.broadcasted_iota(jnp.int32, (tm, 1), 0)
    pieces = []
    for g, w in enumerate(POOL_WINDOWS):
        tok = xbuf_ref[POOL_HALO:, g * gw:(g + 1) * gw]
        cnt = jnp.minimum(t + 1, w).astype(F32)
        y = (windows[g] / cnt - tok).astype(BF16)
        pieces.append(jnp.dot(y, pw_ref[g], preferred_element_type=F32))
    mix = (jnp.concatenate(pieces, axis=1) + pb_ref[...]) * ps_ref[...]
    h2 = h + mix
    h_out_ref[...] = h2
    xn = _rms(h2, gf_ref[...])
    xn_out_ref[...] = xn

    xh = xn.astype(BF16)
    xl = (xn - xh.astype(F32)).astype(BF16)
    both = (jnp.dot(xh, wr_ref[...], preferred_element_type=F32)
            + jnp.dot(xl, wr_ref[...], preferred_element_type=F32))
    logits = both[:, :LANES] + both[:, LANES:]
    lane = lax.broadcasted_iota(jnp.int32, logits.shape, 1)
    lg = jnp.where(lane < n_experts, logits, -jnp.inf)
    m1 = jnp.max(lg, axis=-1, keepdims=True)
    i1 = jnp.min(jnp.where(lg == m1, lane, LANES), axis=-1, keepdims=True)
    lg2 = jnp.where(lane == i1, -jnp.inf, lg)
    m2 = jnp.max(lg2, axis=-1, keepdims=True)
    i2 = jnp.min(jnp.where(lg2 == m2, lane, LANES), axis=-1, keepdims=True)
    e = jnp.exp(m2 - m1)
    w1 = 1.0 / (1.0 + e)
    w2 = e / (1.0 + e)

    sel1 = lane == i1
    sel2 = lane == i2
    onehot = jnp.where(sel1 | sel2, 1.0, 0.0)
    r = lax.broadcasted_iota(jnp.int32, (tm, tm), 0)
    c = lax.broadcasted_iota(jnp.int32, (tm, tm), 1)
    strict_lower = jnp.where(c < r, 1.0, 0.0).astype(BF16)
    before = jnp.dot(strict_lower, onehot.astype(BF16), preferred_element_type=F32) + carry_ref[0:1, :]
    rank1 = jnp.sum(jnp.where(sel1, before, 0.0), axis=-1, keepdims=True)
    rank2 = jnp.sum(jnp.where(sel2, before, 0.0), axis=-1, keepdims=True)
    total = carry_ref[0:1, :] + jnp.sum(onehot, axis=0, keepdims=True)
    carry_ref[...] = jnp.broadcast_to(total, carry_ref.shape)
    cnt_ref[...] = jnp.broadcast_to(total, cnt_ref.shape)

    info = jnp.where(lane == 0, i1.astype(F32), 0.0)
    info = jnp.where(lane == 1, i2.astype(F32), info)
    info = jnp.where(lane == 2, w1, info)
    info = jnp.where(lane == 3, w2, info)
    info = jnp.where(lane == 4, rank1, info)
    info = jnp.where(lane == 5, rank2, info)
    info_ref[...] = info


def _pool_router(h, S, g_mix, pool_w, pool_b, pool_scale, g_ffn, w_router_pad, n_experts):
    T, D = h.shape
    tm = ROW_TILE
    G, gw, _ = pool_w.shape
    halo_per_tile = tm // POOL_HALO
    row = pl.BlockSpec((tm, D), lambda i: (i, 0))
    return pl.pallas_call(
        functools.partial(_pool_router_kernel, blocks_per_seq=S // tm, n_experts=n_experts),
        grid=(T // tm,),
        in_specs=[
            row,
            pl.BlockSpec((POOL_HALO, D), lambda i: (jnp.maximum(i * halo_per_tile - 1, 0), 0)),
            _const_spec((1, D)), _const_spec((G, gw, gw)), _const_spec((1, D)), _const_spec((1, D)),
            _const_spec((1, D)), _const_spec((D, 2 * LANES)),
        ],
        out_specs=[row, row,
                   pl.BlockSpec((tm, LANES), lambda i: (i, 0)),
                   pl.BlockSpec((8, LANES), lambda i: (0, 0))],
        out_shape=[jax.ShapeDtypeStruct((T, D), F32), jax.ShapeDtypeStruct((T, D), F32),
                   jax.ShapeDtypeStruct((T, LANES), F32), jax.ShapeDtypeStruct((8, LANES), F32)],
        scratch_shapes=[pltpu.VMEM((tm + POOL_HALO, D), F32), pltpu.VMEM((8, LANES), F32),
                        pltpu.VMEM((tm + POOL_HALO, D), F32), pltpu.VMEM((tm + POOL_HALO, D), F32)],
        compiler_params=_params("arbitrary"),
        name="pool_router",
    )(h, h, g_mix, pool_w, pool_b, pool_scale, g_ffn, w_router_pad)


def _scatter_kernel(pos_ref, pad_ref, blk_ref, inv_init_ref, x_ref, out_ref, inv_ref, sem):
    n = x_ref.shape[0]
    n_tokens = n * pl.num_programs(0)
    first = pl.program_id(0) * n

    def copy(src_row, n_rows, dst_row):
        return pltpu.make_async_copy(x_ref.at[pl.ds(src_row, n_rows)], out_ref.at[pl.ds(dst_row, n_rows)], sem)

    def issue(group, carry):
        r0 = pl.multiple_of(group * SUBLANES, SUBLANES)
        for q in range(SUBLANES):
            for k in range(TOP_K):
                p = pos_ref[0, k, r0 + q]
                copy(r0 + q, 1, p).start()
                inv_ref[EXPERT_TILE + p] = k * n_tokens + first + r0 + q
        return carry

    lax.fori_loop(0, n // SUBLANES, issue, 0)

    @pl.when(pl.program_id(0) == pl.num_programs(0) - 1)
    def _():
        def keep_lead(r, carry):
            inv_ref[r] = inv_init_ref[r]
            return carry

        lax.fori_loop(0, EXPERT_TILE, keep_lead, 0, unroll=8)

        for e in range(pad_ref.shape[1]):
            start = pad_ref[0, e]
            count = pad_ref[1, e]

            def fill(r, carry):
                copy(0, 1, start + r).start()
                inv_ref[EXPERT_TILE + start + r] = inv_init_ref[EXPERT_TILE + start + r]
                return carry

            lax.fori_loop(0, count, fill, 0)

            whole = pl.multiple_of(count // 8 * 8, 8)

            @pl.when(whole > 0)
            def _():
                copy(0, whole, 0).wait()

            def drain(r, carry):
                copy(0, 1, 0).wait()
                return carry

            lax.fori_loop(0, count - whole, drain, 0)

        for k in range(blk_ref.shape[0]):
            blk = blk_ref[k]

            @pl.when(blk >= 0)
            def _():
                c = copy(0, EXPERT_SUB, pl.multiple_of(blk * EXPERT_SUB, EXPERT_SUB))
                c.start()

                def keep(r, carry):
                    inv_ref[EXPERT_TILE + blk * EXPERT_SUB + r] = inv_init_ref[EXPERT_TILE + blk * EXPERT_SUB + r]
                    return carry

                lax.fori_loop(0, EXPERT_SUB, keep, 0, unroll=8)
                c.wait()

    for k in range(TOP_K):
        copy(0, n, 0).wait()


def _row_scatter(x, pos, pad, empty_blocks, n_out):
    T, D = x.shape
    n = GATHER_ROWS
    assert EXPERT_SUB <= n
    pos_blocks = pos.reshape(T // n, n, TOP_K).transpose(0, 2, 1)
    pad_dst = TOP_K * T + jnp.arange(n_out + EXPERT_TILE, dtype=jnp.int32) % EXPERT_TILE
    return pl.pallas_call(
        _scatter_kernel,
        grid=(T // n,),
        in_specs=[pl.BlockSpec((1, TOP_K, n), lambda i: (i, 0, 0), memory_space=pltpu.SMEM),
                  pl.BlockSpec(memory_space=pltpu.SMEM),
                  pl.BlockSpec(memory_space=pltpu.SMEM),
                  pl.BlockSpec(memory_space=pltpu.SMEM),
                  pl.BlockSpec((n, D), lambda i: (i, 0))],
        out_specs=[pl.BlockSpec(memory_space=pl.ANY), pl.BlockSpec(memory_space=pltpu.SMEM)],
        out_shape=[jax.ShapeDtypeStruct((n_out, D), x.dtype),
                   jax.ShapeDtypeStruct((n_out + EXPERT_TILE,), jnp.int32)],
        scratch_shapes=[pltpu.SemaphoreType.DMA],
        compiler_params=_params("arbitrary"),
        name="row_scatter",
    )(pos_blocks, pad, empty_blocks, pad_dst, x)


def _expert_kernel(te_ref, tr_ref, tx_ref, inv_ref, x_ref, wg_ref, wu_ref, wd_ref, yk_ref, yo_ref, sem,
                   *, n_f):
    i = pl.program_id(0)
    j = pl.program_id(1)
    tm, D = x_ref.shape
    sub = EXPERT_SUB
    chunk = tm // n_f // SUBLANES * SUBLANES
    rest = tm - n_f * chunk
    slot = i % 2
    prev = 1 - slot
    n_live = (tr_ref[i] + sub - 1) // sub

    def row_copy(buf, row, dst):
        return pltpu.make_async_copy(yo_ref.at[buf, pl.ds(row, 1)], yk_ref.at[pl.ds(dst, 1)], sem.at[buf])

    def wait_rows(buf):
        pltpu.make_async_copy(yo_ref.at[buf], yk_ref.at[pl.ds(0, tm)], sem.at[buf]).wait()

    def send_prev_rows(first_row, count):
        first_row = pl.multiple_of(first_row, SUBLANES)
        table = i * tm + first_row
        for q in range(count):
            row_copy(prev, first_row + q, inv_ref[table + q]).start()

    @pl.when((i == 0) & (j == 0))
    def _():
        yo_ref[...] = jnp.zeros(yo_ref.shape, F32)

    def sends_rows(t):
        return (t == 0) | (tr_ref[jnp.maximum(t - 1, 0)] > 0)

    @pl.when((i > 0) & (j == 0) & sends_rows(i - 1))
    def _():
        wait_rows(slot)

    for n in range(1, tm // sub + 1):
        m = n * sub

        @pl.when(n_live == n)
        def _():
            send_prev_rows(j * chunk, chunk)
            x = x_ref[:m, :].astype(BF16)
            part = _swiglu_chunks(x, wg_ref.at[0], wu_ref.at[0], wd_ref.at[0], MXU_COLS)
            yo_ref[slot, :m, :] = jnp.where(j > 0, yo_ref[slot, :m, :], 0.0) + part

            @pl.when(j == 0)
            def _():
                send_prev_rows(n_f * chunk, rest)
                if m < tm:
                    yo_ref[slot, m:, :] = jnp.zeros((tm - m, D), F32)

    @pl.when((n_live == 0) & sends_rows(i))
    def _():
        send_prev_rows(j * chunk, chunk)

        @pl.when(j == 0)
        def _():
            send_prev_rows(n_f * chunk, rest)

    @pl.when((i == pl.num_programs(0) - 1) & (j == n_f - 1) & sends_rows(i))
    def _():
        wait_rows(prev)


def _experts(xs, inv, tile_expert, tile_rows, tile_src, w_gate, w_up, w_down, n_tokens):
    P = xs.shape[0]
    E, D, F = w_gate.shape
    tm, tf = EXPERT_TILE, EXPERT_F_TILE
    n_f = F // tf
    dump_base = TOP_K * n_tokens
    dump_rows = tm

    def row_map(i, j, te, tr, tx, inv):
        return (tx[i], 0)

    def col_w(i, j, te, tr, tx, inv):
        return (te[i], 0, jnp.where(tr[i] > 0, j, n_f - 1))

    def row_w(i, j, te, tr, tx, inv):
        return (te[i], jnp.where(tr[i] > 0, j, n_f - 1), 0)

    grid_spec = pltpu.PrefetchScalarGridSpec(
        num_scalar_prefetch=4,
        grid=(P // tm + 1, n_f),
        in_specs=[pl.BlockSpec((tm, D), row_map),
                  pl.BlockSpec((1, D, tf), col_w),
                  pl.BlockSpec((1, D, tf), col_w),
                  pl.BlockSpec((1, tf, D), row_w)],
        out_specs=pl.BlockSpec(memory_space=pl.ANY),
        scratch_shapes=[pltpu.VMEM((2, tm, D), F32), pltpu.SemaphoreType.DMA((2,))],
    )
    return pl.pallas_call(
        functools.partial(_expert_kernel, n_f=n_f),
        grid_spec=grid_spec,
        out_shape=jax.ShapeDtypeStruct((dump_base + dump_rows, D), F32),
        compiler_params=_params("arbitrary", "arbitrary"),
        name="expert_swiglu",
    )(tile_expert, tile_rows, tile_src, inv, xs, w_gate, w_up, w_down)


def _combine_kernel(h_ref, y1_ref, y2_ref, info_ref, p_ref, gp_ref, wpg_ref, wpp_ref, o_ref):
    info = info_ref[...]
    h = h_ref[...] + info[:, 2:3] * y1_ref[...] + info[:, 3:4] * y2_ref[...]
    o_ref[...] = _ple(h, p_ref[...], gp_ref[...], wpg_ref[...], wpp_ref[...])


def _combine(h, yk, info, p3, layer, g_ple, w_pg, w_pp):
    T, D = h.shape
    PD = p3.shape[2]
    tm = COMBINE_TILE
    nb = T // tm
    row = lambda w: pl.BlockSpec((tm, w), lambda i: (i, 0))
    return pl.pallas_call(
        _combine_kernel,
        grid=(nb,),
        in_specs=[row(D), row(D), pl.BlockSpec((tm, D), lambda i: (nb + i, 0)), row(LANES),
                  pl.BlockSpec((None, tm, PD), lambda i: (layer, i, 0)),
                  _const_spec((1, D)), _const_spec((D, D)), _const_spec((PD, D))],
        out_specs=row(D),
        out_shape=jax.ShapeDtypeStruct((T, D), F32),
        compiler_params=_params("parallel"),
        name="combine_ple",
    )(h, yk, yk, info, p3, g_ple, w_pg, w_pp)


def _rope_tables(S):
    inv = 1.0 / (ROPE_THETA ** (jnp.arange(0, DH, 2, dtype=F32) / DH))
    ang = jnp.arange(S, dtype=F32)[:, None] * inv[None, :]
    cos, sin = jnp.cos(ang), jnp.sin(ang)
    reps = HEAD_W // DH
    cos_t = jnp.tile(jnp.concatenate([cos, cos], axis=1), (1, reps))
    sin_t = jnp.tile(jnp.concatenate([-sin, sin], axis=1), (1, reps))
    return cos_t, sin_t


def _routing_tables(info, counts_f, n_experts, T):
    tm = EXPERT_TILE
    e12 = info[:, 0:2].astype(jnp.int32)
    rank = info[:, 4:6].astype(jnp.int32)
    counts = counts_f[0, :n_experts].astype(jnp.int32)
    padded = (counts + tm - 1) // tm * tm
    ends = jnp.cumsum(padded)
    starts = ends - padded
    onehot = e12[:, :, None] == jnp.arange(n_experts, dtype=jnp.int32)
    pos = jnp.sum(jnp.where(onehot, starts, 0), axis=-1) + rank
    P = T * TOP_K + n_experts * tm
    sub = EXPERT_SUB
    pad = jnp.stack([starts + counts, (counts + sub - 1) // sub * sub - counts])
    blk_start = jnp.arange(P // sub, dtype=jnp.int32) * sub
    holds_rows = jnp.any((blk_start[:, None] >= starts[None, :]) & (blk_start[:, None] < (starts + counts)[None, :]), axis=1)
    slot = jnp.where(holds_rows, -1, jnp.cumsum(~holds_rows) - 1)
    hit = slot[None, :] == jnp.arange(n_experts * tm // sub, dtype=jnp.int32)[:, None]
    empty_blocks = (jnp.sum(jnp.where(hit, jnp.arange(P // sub, dtype=jnp.int32) + 1, 0), axis=1) - 1).astype(jnp.int32)
    n_tiles = P // tm + 1
    tile_start = jnp.arange(n_tiles, dtype=jnp.int32) * tm
    tile_expert = jnp.minimum(jnp.sum(tile_start[:, None] >= ends[None, :], axis=1), n_experts - 1).astype(jnp.int32)
    tile_rows = jnp.clip((starts + counts)[tile_expert] - tile_start, 0, tm).astype(jnp.int32)
    tile_src = jnp.minimum(jnp.arange(n_tiles, dtype=jnp.int32), ends[-1] // tm - 1).astype(jnp.int32)
    return pos, pad, empty_blocks, P, tile_expert, tile_rows, tile_src


def kernel(x, p, norm_mix, norm_ffn, norm_ple, attn_w_qkv, attn_w_o, attn_q_norm, attn_k_norm,
           attn_lambda_q1, attn_lambda_k1, attn_lambda_q2, attn_lambda_k2, attn_subln,
           pool_w, pool_b, pool_scale, ffn_w_gate, ffn_w_up, ffn_w_down,
           moe_router, moe_w_gate, moe_w_up, moe_w_down, ple_w_proj, ple_w_gate):
    B, S, D = x.shape
    T = B * S
    n_experts = moe_router.shape[-1]
    x2 = x.reshape(T, D)
    p3 = p.reshape(p.shape[0], T, p.shape[-1])
    bf = lambda a: a.astype(BF16)
    vec = lambda a: a.reshape(1, -1)

    lam_init = 0.8 - 0.6 * math.exp(-0.3 * 0)
    cos_t, sin_t = _rope_tables(S)
    reps = HEAD_W // DH
    qk_gain = jnp.zeros((8, LANES), F32)
    qk_gain = qk_gain.at[0].set(jnp.tile(attn_q_norm[0], reps) * (ATTN_SCALE * math.log2(math.e)))
    qk_gain = qk_gain.at[1].set(jnp.tile(attn_k_norm[0], reps))
    group = jnp.arange(2 * LANES) // DH
    ones = (group[:, None] == group[None, :]).astype(BF16)
    qkv = _qkv(x2, vec(norm_mix[0]), attn_w_qkv[0], cos_t, sin_t, qk_gain, ones)
    lam_params = jnp.stack([attn_lambda_q1[0], attn_lambda_k1[0], attn_lambda_q2[0], attn_lambda_k2[0]])
    ao = _attention(qkv.reshape(B, S, 3 * D), lam_params, vec(attn_subln[0]), lam_init)
    h = _dense_tail(ao.reshape(T, D), x2, p3, 0, bf(attn_w_o[0]), vec(norm_ffn[0]),
                    bf(ffn_w_gate[0]), bf(ffn_w_up[0]), bf(ffn_w_down[0]),
                    vec(norm_ple[0]), bf(ple_w_gate[0]), bf(ple_w_proj[0]))

    w_router = jnp.zeros((D, LANES), F32).at[:, :n_experts].set(moe_router[0])
    w_router_hi = w_router.astype(BF16)
    w_router_pad = jnp.concatenate([w_router_hi, (w_router - w_router_hi.astype(F32)).astype(BF16)], axis=1)
    h, xn, info, counts = _pool_router(h, S, vec(norm_mix[1]), bf(pool_w[0]), vec(pool_b[0]),
                                       vec(pool_scale[0]), vec(norm_ffn[1]), w_router_pad, n_experts)
    pos, pad, empty_blocks, n_sorted, tile_expert, tile_rows, tile_src = _routing_tables(info, counts, n_experts, T)
    xs, inv = _row_scatter(xn, pos, pad, empty_blocks, n_sorted)
    yk = _experts(xs, inv, tile_expert, tile_rows, tile_src, moe_w_gate[0], moe_w_up[0], moe_w_down[0], T)
    out = _combine(h, yk, info, p3, 1, vec(norm_ple[1]), bf(ple_w_gate[1]), bf(ple_w_proj[1]))
    return out.reshape(B, S, D)
```

```python
import functools
import math

import jax
import jax.numpy as jnp
from jax import lax
from jax.experimental import pallas as pl
from jax.experimental.pallas import tpu as pltpu

F32 = jnp.float32
BF16 = jnp.bfloat16

DH = 64
HEAD_W = 2 * DH
ATTN_SCALE = 1.0 / math.sqrt(DH)
ROPE_THETA = 10000.0
POOL_WINDOWS = (2, 4, 8, 16)
POOL_HALO = 32
TOP_K = 2
RMS_EPS = 1e-6
LANES = 128
SUBLANES = 8
MXU_COLS = 256
DENSE_F_CHUNK = 512
VMEM_LIMIT = 56 * 1024 * 1024

ROW_TILE = 512
COMBINE_TILE = 1024
Q_TILE = 256
ATTN_HEADS_PER_STEP = 2
EXPERT_TILE = 1024
EXPERT_SUB = 256
EXPERT_F_TILE = 512
GATHER_ROWS = 2048


def _rms(x, gain):
    return x * lax.rsqrt(jnp.mean(x * x, axis=-1, keepdims=True) + RMS_EPS) * gain


def _const_spec(shape):
    zeros = (0,) * len(shape)
    return pl.BlockSpec(shape, lambda *_: zeros, pipeline_mode=pl.Buffered(1))


def _params(*sem):
    return pltpu.CompilerParams(dimension_semantics=sem, vmem_limit_bytes=VMEM_LIMIT)


def _qkv_kernel(x_ref, g_ref, w_ref, cos_ref, sin_ref, qkg_ref, ones_ref, o_ref, wb_ref):
    D = x_ref.shape[1]

    @pl.when(pl.program_id(0) == 0)
    def _():
        wb_ref[...] = w_ref[...].astype(BF16)

    xn = _rms(x_ref[...], g_ref[...]).astype(BF16)
    cos = cos_ref[...]
    sin = sin_ref[...]
    lane = lax.broadcasted_iota(jnp.int32, cos.shape, 1)
    first_half = (lane % DH) < (DH // 2)
    ones = ones_ref[...]
    width = ones.shape[0]
    def project(col):
        return jnp.dot(xn, wb_ref[:, col:col + width], preferred_element_type=F32)

    def finish(col, yc):
        part = col // D
        if part == 2:
            o_ref[:, col:col + width] = yc.astype(BF16)
            return
        gain = qkg_ref[part:part + 1, :]
        ss = jnp.dot((yc * yc).astype(BF16), ones, preferred_element_type=F32)
        yn = yc * lax.rsqrt(ss * (1.0 / DH) + RMS_EPS)
        for s in range(0, width, LANES):
            z = yn[:, s:s + LANES] * gain
            rot = jnp.where(first_half,
                            pltpu.roll(z, LANES - DH // 2, 1),
                            pltpu.roll(z, DH // 2, 1))
            o_ref[:, col + s:col + s + LANES] = (z * cos + rot * sin).astype(BF16)

    cols = list(range(0, 3 * D, width))
    pending = project(cols[0])
    for n, col in enumerate(cols):
        upcoming = project(cols[n + 1]) if n + 1 < len(cols) else None
        finish(col, pending)
        pending = upcoming


def _qkv(x2, gain, w_qkv, cos_t, sin_t, qk_gain, ones):
    T, D = x2.shape
    S = cos_t.shape[0]
    tm = ROW_TILE
    blocks_per_seq = S // tm
    return pl.pallas_call(
        _qkv_kernel,
        grid=(T // tm,),
        in_specs=[
            pl.BlockSpec((tm, D), lambda i: (i, 0)),
            _const_spec((1, D)),
            _const_spec((D, 3 * D)),
            pl.BlockSpec((tm, LANES), lambda i: (i % blocks_per_seq, 0)),
            pl.BlockSpec((tm, LANES), lambda i: (i % blocks_per_seq, 0)),
            _const_spec(qk_gain.shape),
            _const_spec(ones.shape),
        ],
        out_specs=pl.BlockSpec((tm, 3 * D), lambda i: (i, 0)),
        out_shape=jax.ShapeDtypeStruct((T, 3 * D), BF16),
        scratch_shapes=[pltpu.VMEM((D, 3 * D), BF16)],
        compiler_params=_params("arbitrary"),
        name="qkv_rope",
    )(x2, gain, w_qkv, cos_t, sin_t, qk_gain, ones)


def _dot_nt(a, b):
    return lax.dot_general(a, b, (((1,), (1,)), ((), ())), preferred_element_type=F32)


def _attn_kernel(q_ref, k_ref, v_ref, lam_ref, g_ref, o_ref, vext_ref, *, lam_init):
    lp = lam_ref[...]
    lam = (jnp.exp(jnp.sum(lp[0:1] * lp[1:2], axis=-1, keepdims=True))
           - jnp.exp(jnp.sum(lp[2:3] * lp[3:4], axis=-1, keepdims=True)) + lam_init)
    S = q_ref.shape[1]
    tq = Q_TILE
    vlane = lax.broadcasted_iota(jnp.int32, (S, HEAD_W), 1)
    lane = lax.broadcasted_iota(jnp.int32, (tq, HEAD_W), 1)
    row = lax.broadcasted_iota(jnp.int32, (tq, tq), 0)
    col = lax.broadcasted_iota(jnp.int32, (tq, tq), 1)
    future = col > row
    gain = g_ref[...]
    n_heads = q_ref.shape[2] // HEAD_W
    for head in range(n_heads):
        vext_ref[head, :, :HEAD_W] = v_ref[0, :, head * HEAD_W:(head + 1) * HEAD_W]
        vext_ref[head, :, HEAD_W:] = jnp.where(vlane == 0, 1.0, 0.0).astype(BF16)

    def scores(head, i):
        cols = slice(head * HEAD_W, (head + 1) * HEAD_W)
        q = q_ref[0, i * tq:(i + 1) * tq, cols]
        out = []
        for qc in (jnp.where(lane < DH, q, jnp.zeros_like(q)), jnp.where(lane >= DH, q, jnp.zeros_like(q))):
            s_diag = jnp.where(future, -jnp.inf, _dot_nt(qc, k_ref[0, i * tq:(i + 1) * tq, cols]))
            s_off = _dot_nt(qc, k_ref[0, :i * tq, cols]) if i > 0 else None
            out.append((s_off, s_diag))
        return out

    def finish(head, i, maps):
        kv = (i + 1) * tq
        acc = []
        for s_off, s_diag in maps:
            m = jnp.max(s_diag, axis=-1, keepdims=True)
            if s_off is not None:
                m = jnp.maximum(m, jnp.max(s_off, axis=-1, keepdims=True))
                p = jnp.concatenate([jnp.exp2(s_off - m).astype(BF16),
                                     jnp.exp2(s_diag - m).astype(BF16)], axis=1)
            else:
                p = jnp.exp2(s_diag - m).astype(BF16)
            acc.append(jnp.dot(p, vext_ref[head, :kv, :], preferred_element_type=F32))
        o = (acc[0][:, :HEAD_W] * (1.0 / acc[0][:, HEAD_W:HEAD_W + 1])
             - acc[1][:, :HEAD_W] * (lam / acc[1][:, HEAD_W:HEAD_W + 1]))
        o = _rms(o, gain) * (1.0 - lam_init)
        o_ref[0, i * tq:(i + 1) * tq, head * HEAD_W:(head + 1) * HEAD_W] = o.astype(BF16)

    work = [(head, i) for head in range(n_heads) for i in range(S // tq)]
    pending = scores(*work[0])
    for pos, item in enumerate(work):
        upcoming = scores(*work[pos + 1]) if pos + 1 < len(work) else None
        finish(*item, pending)
        pending = upcoming


def _attention(qkv3, lam_params, subln, lam_init):
    B, S, W3 = qkv3.shape
    heads = ATTN_HEADS_PER_STEP
    H = W3 // 3 // (heads * HEAD_W)
    blk = (1, S, heads * HEAD_W)
    return pl.pallas_call(
        functools.partial(_attn_kernel, lam_init=lam_init),
        grid=(B, H),
        in_specs=[
            pl.BlockSpec(blk, lambda b, h: (b, 0, h)),
            pl.BlockSpec(blk, lambda b, h: (b, 0, H + h)),
            pl.BlockSpec(blk, lambda b, h: (b, 0, 2 * H + h)),
            _const_spec(lam_params.shape),
            _const_spec(subln.shape),
        ],
        out_specs=pl.BlockSpec(blk, lambda b, h: (b, 0, h)),
        out_shape=jax.ShapeDtypeStruct((B, S, W3 // 3), BF16),
        scratch_shapes=[pltpu.VMEM((heads, S, 2 * HEAD_W), BF16)],
        compiler_params=_params("parallel", "parallel"),
        name="diff_attn",
    )(qkv3, qkv3, qkv3, lam_params, subln)


def _ple(h, p, g_ple, w_gate, w_proj):
    xn = _rms(h, g_ple).astype(BF16)
    gate = jax.nn.sigmoid(jnp.dot(xn, w_gate, preferred_element_type=F32))
    proj = jnp.dot(p.astype(BF16), w_proj, preferred_element_type=F32)
    return h + gate * proj


def _swiglu_chunks(xn, wg_ref, wu_ref, wd_ref, width):
    F = wg_ref.shape[1]
    chunks = [(lo, min(lo + width, F)) for lo in range(0, F, width)]

    def gate_up(lo, hi):
        return (jnp.dot(xn, wg_ref[:, lo:hi].astype(BF16), preferred_element_type=F32),
                jnp.dot(xn, wu_ref[:, lo:hi].astype(BF16), preferred_element_type=F32))

    out = None
    pending = gate_up(*chunks[0])
    for c, (lo, hi) in enumerate(chunks):
        upcoming = gate_up(*chunks[c + 1]) if c + 1 < len(chunks) else None
        g, u = pending
        act = (g * jax.nn.sigmoid(g) * u).astype(BF16)
        part = jnp.dot(act, wd_ref[lo:hi, :].astype(BF16), preferred_element_type=F32)
        out = part if out is None else out + part
        pending = upcoming
    return out


def _dense_tail_kernel(ao_ref, x_ref, p_ref, wo_ref, gf_ref, wg_ref, wu_ref, wd_ref,
                       gp_ref, wpg_ref, wpp_ref, o_ref, *, f_chunk):
    h = x_ref[...] + jnp.dot(ao_ref[...], wo_ref[...], preferred_element_type=F32)
    xn = _rms(h, gf_ref[...]).astype(BF16)
    acc = h + _swiglu_chunks(xn, wg_ref, wu_ref, wd_ref, f_chunk)
    o_ref[...] = _ple(acc, p_ref[...], gp_ref[...], wpg_ref[...], wpp_ref[...])


def _dense_tail(ao, x2, p3, layer, w_o, g_ffn, w_gate, w_up, w_down, g_ple, w_pg, w_pp):
    T, D = x2.shape
    F = w_gate.shape[1]
    PD = p3.shape[2]
    tm = ROW_TILE
    row = lambda w: pl.BlockSpec((tm, w), lambda i: (i, 0))
    return pl.pallas_call(
        functools.partial(_dense_tail_kernel, f_chunk=DENSE_F_CHUNK),
        grid=(T // tm,),
        in_specs=[row(D), row(D), pl.BlockSpec((None, tm, PD), lambda i: (layer, i, 0)),
                  _const_spec((D, D)), _const_spec((1, D)),
                  _const_spec((D, F)), _const_spec((D, F)), _const_spec((F, D)),
                  _const_spec((1, D)), _const_spec((D, D)), _const_spec((PD, D))],
        out_specs=row(D),
        out_shape=jax.ShapeDtypeStruct((T, D), F32),
        compiler_params=_params("parallel"),
        name="dense_tail",
    )(ao, x2, p3, w_o, g_ffn, w_gate, w_up, w_down, g_ple, w_pg, w_pp)


def _pool_router_kernel(h_ref, halo_ref, gm_ref, pw_ref, pb_ref, ps_ref, gf_ref, wr_ref,
                        h_out_ref, xn_out_ref, info_ref, cnt_ref, xbuf_ref, carry_ref, *stage_refs,
                        blocks_per_seq, n_experts):
    i = pl.program_id(0)
    tm, D = h_ref.shape
    gw = D // len(POOL_WINDOWS)
    seq_block = i % blocks_per_seq

    @pl.when(i == 0)
    def _():
        carry_ref[...] = jnp.zeros_like(carry_ref)

    h = h_ref[...]
    gm = gm_ref[...]
    xbuf_ref[POOL_HALO:, :] = _rms(h, gm)

    @pl.when(seq_block == 0)
    def _():
        xbuf_ref[:POOL_HALO, :] = jnp.zeros((POOL_HALO, D), F32)

    @pl.when(seq_block != 0)
    def _():
        xbuf_ref[:POOL_HALO, :] = _rms(halo_ref[...], gm)

    n_rows = tm + POOL_HALO
    src, lo = xbuf_ref, 0
    windows = []
    for g, w in enumerate(POOL_WINDOWS):
        shift = w // 2
        assert w == 2 << g
        lo = -(-(lo + shift) // SUBLANES) * SUBLANES
        assert lo <= POOL_HALO
        val = src[lo:n_rows, g * gw:] + src[lo - shift:n_rows - shift, g * gw:]
        if g + 1 < len(POOL_WINDOWS):
            dst = stage_refs[g % 2]
            dst[lo:n_rows, g * gw:] = val
            windows.append(dst[POOL_HALO:n_rows, g * gw:(g + 1) * gw])
            src = dst
        else:
            windows.append(val[POOL_HALO - lo:, :])

    t = seq_block * tm + lax.broadcasted_iota(jnp.int32, (tm, 1), 0)
    pieces = []
    for g, w in enumerate(POOL_WINDOWS):
        tok = xbuf_ref[POOL_HALO:, g * gw:(g + 1) * gw]
        cnt = jnp.minimum(t + 1, w).astype(F32)
        y = (windows[g] / cnt - tok).astype(BF16)
        pieces.append(jnp.dot(y, pw_ref[g], preferred_element_type=F32))
    mix = (jnp.concatenate(pieces, axis=1) + pb_ref[...]) * ps_ref[...]
    h2 = h + mix
    h_out_ref[...] = h2
    xn = _rms(h2, gf_ref[...])
    xn_out_ref[...] = xn

    xh = xn.astype(BF16)
    xl = (xn - xh.astype(F32)).astype(BF16)
    both = (jnp.dot(xh, wr_ref[...], preferred_element_type=F32)
            + jnp.dot(xl, wr_ref[...], preferred_element_type=F32))
    logits = both[:, :LANES] + both[:, LANES:]
    lane = lax.broadcasted_iota(jnp.int32, logits.shape, 1)
    lg = jnp.where(lane < n_experts, logits, -jnp.inf)
    m1 = jnp.max(lg, axis=-1, keepdims=True)
    i1 = jnp.min(jnp.where(lg == m1, lane, LANES), axis=-1, keepdims=True)
    lg2 = jnp.where(lane == i1, -jnp.inf, lg)
    m2 = jnp.max(lg2, axis=-1, keepdims=True)
    i2 = jnp.min(jnp.where(lg2 == m2, lane, LANES), axis=-1, keepdims=True)
    e = jnp.exp(m2 - m1)
    w1 = 1.0 / (1.0 + e)
    w2 = e / (1.0 + e)

    sel1 = lane == i1
    sel2 = lane == i2
    onehot = jnp.where(sel1 | sel2, 1.0, 0.0)
    r = lax.broadcasted_iota(jnp.int32, (tm, tm), 0)
    c = lax.broadcasted_iota(jnp.int32, (tm, tm), 1)
    strict_lower = jnp.where(c < r, 1.0, 0.0).astype(BF16)
    before = jnp.dot(strict_lower, onehot.astype(BF16), preferred_element_type=F32) + carry_ref[0:1, :]
    rank1 = jnp.sum(jnp.where(sel1, before, 0.0), axis=-1, keepdims=True)
    rank2 = jnp.sum(jnp.where(sel2, before, 0.0), axis=-1, keepdims=True)
    total = carry_ref[0:1, :] + jnp.sum(onehot, axis=0, keepdims=True)
    carry_ref[...] = jnp.broadcast_to(total, carry_ref.shape)
    cnt_ref[...] = jnp.broadcast_to(total, cnt_ref.shape)

    info = jnp.where(lane == 0, i1.astype(F32), 0.0)
    info = jnp.where(lane == 1, i2.astype(F32), info)
    info = jnp.where(lane == 2, w1, info)
    info = jnp.where(lane == 3, w2, info)
    info = jnp.where(lane == 4, rank1, info)
    info = jnp.where(lane == 5, rank2, info)
    info_ref[...] = info


def _pool_router(h, S, g_mix, pool_w, pool_b, pool_scale, g_ffn, w_router_pad, n_experts):
    T, D = h.shape
    tm = ROW_TILE
    G, gw, _ = pool_w.shape
    halo_per_tile = tm // POOL_HALO
    row = pl.BlockSpec((tm, D), lambda i: (i, 0))
    return pl.pallas_call(
        functools.partial(_pool_router_kernel, blocks_per_seq=S // tm, n_experts=n_experts),
        grid=(T // tm,),
        in_specs=[
            row,
            pl.BlockSpec((POOL_HALO, D), lambda i: (jnp.maximum(i * halo_per_tile - 1, 0), 0)),
            _const_spec((1, D)), _const_spec((G, gw, gw)), _const_spec((1, D)), _const_spec((1, D)),
            _const_spec((1, D)), _const_spec((D, 2 * LANES)),
        ],
        out_specs=[row, row,
                   pl.BlockSpec((tm, LANES), lambda i: (i, 0)),
                   pl.BlockSpec((8, LANES), lambda i: (0, 0))],
        out_shape=[jax.ShapeDtypeStruct((T, D), F32), jax.ShapeDtypeStruct((T, D), F32),
                   jax.ShapeDtypeStruct((T, LANES), F32), jax.ShapeDtypeStruct((8, LANES), F32)],
        scratch_shapes=[pltpu.VMEM((tm + POOL_HALO, D), F32), pltpu.VMEM((8, LANES), F32),
                        pltpu.VMEM((tm + POOL_HALO, D), F32), pltpu.VMEM((tm + POOL_HALO, D), F32)],
        compiler_params=_params("arbitrary"),
        name="pool_router",
    )(h, h, g_mix, pool_w, pool_b, pool_scale, g_ffn, w_router_pad)


def _scatter_kernel(pos_ref, pad_ref, blk_ref, inv_init_ref, x_ref, out_ref, inv_ref, sem):
    n = x_ref.shape[0]
    n_tokens = n * pl.num_programs(0)
    first = pl.program_id(0) * n

    def copy(src_row, n_rows, dst_row):
        return pltpu.make_async_copy(x_ref.at[pl.ds(src_row, n_rows)], out_ref.at[pl.ds(dst_row, n_rows)], sem)

    def issue(group, carry):
        r0 = pl.multiple_of(group * SUBLANES, SUBLANES)
        for q in range(SUBLANES):
            for k in range(TOP_K):
                p = pos_ref[0, k, r0 + q]
                copy(r0 + q, 1, p).start(priority=k)
                inv_ref[EXPERT_TILE + p] = k * n_tokens + first + r0 + q
        return carry

    lax.fori_loop(0, n // SUBLANES, issue, 0)

    @pl.when(pl.program_id(0) == pl.num_programs(0) - 1)
    def _():
        def keep_lead(r, carry):
            inv_ref[r] = inv_init_ref[r]
            return carry

        lax.fori_loop(0, EXPERT_TILE, keep_lead, 0, unroll=8)

        for e in range(pad_ref.shape[1]):
            start = pad_ref[0, e]
            count = pad_ref[1, e]

            def fill(r, carry):
                copy(0, 1, start + r).start()
                inv_ref[EXPERT_TILE + start + r] = inv_init_ref[EXPERT_TILE + start + r]
                return carry

            lax.fori_loop(0, count, fill, 0)

            whole = pl.multiple_of(count // 8 * 8, 8)

            @pl.when(whole > 0)
            def _():
                copy(0, whole, 0).wait()

            def drain(r, carry):
                copy(0, 1, 0).wait()
                return carry

            lax.fori_loop(0, count - whole, drain, 0)

        for k in range(blk_ref.shape[0]):
            blk = blk_ref[k]

            @pl.when(blk >= 0)
            def _():
                c = copy(0, EXPERT_SUB, pl.multiple_of(blk * EXPERT_SUB, EXPERT_SUB))
                c.start()

                def keep(r, carry):
                    inv_ref[EXPERT_TILE + blk * EXPERT_SUB + r] = inv_init_ref[EXPERT_TILE + blk * EXPERT_SUB + r]
                    return carry

                lax.fori_loop(0, EXPERT_SUB, keep, 0, unroll=8)
                c.wait()

    for k in range(TOP_K):
        copy(0, n, 0).wait()


def _row_scatter(x, pos, pad, empty_blocks, n_out):
    T, D = x.shape
    n = GATHER_ROWS
    assert EXPERT_SUB <= n
    pos_blocks = pos.reshape(T // n, n, TOP_K).transpose(0, 2, 1)
    pad_dst = TOP_K * T + jnp.arange(n_out + EXPERT_TILE, dtype=jnp.int32) % EXPERT_TILE
    return pl.pallas_call(
        _scatter_kernel,
        grid=(T // n,),
        in_specs=[pl.BlockSpec((1, TOP_K, n), lambda i: (i, 0, 0), memory_space=pltpu.SMEM),
                  pl.BlockSpec(memory_space=pltpu.SMEM),
                  pl.BlockSpec(memory_space=pltpu.SMEM),
                  pl.BlockSpec(memory_space=pltpu.SMEM),
                  pl.BlockSpec((n, D), lambda i: (i, 0))],
        out_specs=[pl.BlockSpec(memory_space=pl.ANY), pl.BlockSpec(memory_space=pltpu.SMEM)],
        out_shape=[jax.ShapeDtypeStruct((n_out, D), x.dtype),
                   jax.ShapeDtypeStruct((n_out + EXPERT_TILE,), jnp.int32)],
        scratch_shapes=[pltpu.SemaphoreType.DMA],
        compiler_params=_params("arbitrary"),
        name="row_scatter",
    )(pos_blocks, pad, empty_blocks, pad_dst, x)


def _expert_kernel(te_ref, tr_ref, tx_ref, inv_ref, x_ref, wg_ref, wu_ref, wd_ref, yk_ref, yo_ref, sem,
                   *, n_f):
    i = pl.program_id(0)
    j = pl.program_id(1)
    tm, D = x_ref.shape
    sub = EXPERT_SUB
    chunk = tm // n_f // SUBLANES * SUBLANES
    rest = tm - n_f * chunk
    slot = i % 2
    prev = 1 - slot
    n_live = (tr_ref[i] + sub - 1) // sub

    def row_copy(buf, row, dst):
        return pltpu.make_async_copy(yo_ref.at[buf, pl.ds(row, 1)], yk_ref.at[pl.ds(dst, 1)], sem.at[buf])

    def wait_rows(buf):
        pltpu.make_async_copy(yo_ref.at[buf], yk_ref.at[pl.ds(0, tm)], sem.at[buf]).wait()

    def send_prev_rows(first_row, count):
        first_row = pl.multiple_of(first_row, SUBLANES)
        table = i * tm + first_row
        for q in range(count):
            row_copy(prev, first_row + q, inv_ref[table + q]).start(priority=q % 2)

    @pl.when((i == 0) & (j == 0))
    def _():
        yo_ref[...] = jnp.zeros(yo_ref.shape, F32)

    def sends_rows(t):
        return (t == 0) | (tr_ref[jnp.maximum(t - 1, 0)] > 0)

    @pl.when((i > 0) & (j == 0) & sends_rows(i - 1))
    def _():
        wait_rows(slot)

    for n in range(1, tm // sub + 1):
        m = n * sub

        @pl.when(n_live == n)
        def _():
            send_prev_rows(j * chunk, chunk)
            x = x_ref[:m, :].astype(BF16)
            part = _swiglu_chunks(x, wg_ref.at[0], wu_ref.at[0], wd_ref.at[0], MXU_COLS)
            yo_ref[slot, :m, :] = jnp.where(j > 0, yo_ref[slot, :m, :], 0.0) + part

            @pl.when(j == 0)
            def _():
                send_prev_rows(n_f * chunk, rest)
                if m < tm:
                    yo_ref[slot, m:, :] = jnp.zeros((tm - m, D), F32)

    @pl.when((n_live == 0) & sends_rows(i))
    def _():
        send_prev_rows(j * chunk, chunk)

        @pl.when(j == 0)
        def _():
            send_prev_rows(n_f * chunk, rest)

    @pl.when((i == pl.num_programs(0) - 1) & (j == n_f - 1) & sends_rows(i))
    def _():
        wait_rows(prev)


def _experts(xs, inv, tile_expert, tile_rows, tile_src, w_gate, w_up, w_down, n_tokens):
    P = xs.shape[0]
    E, D, F = w_gate.shape
    tm, tf = EXPERT_TILE, EXPERT_F_TILE
    n_f = F // tf
    dump_base = TOP_K * n_tokens
    dump_rows = tm

    def row_map(i, j, te, tr, tx, inv):
        return (tx[i], 0)

    def col_w(i, j, te, tr, tx, inv):
        return (te[i], 0, jnp.where(tr[i] > 0, j, n_f - 1))

    def row_w(i, j, te, tr, tx, inv):
        return (te[i], jnp.where(tr[i] > 0, j, n_f - 1), 0)

    grid_spec = pltpu.PrefetchScalarGridSpec(
        num_scalar_prefetch=4,
        grid=(P // tm + 1, n_f),
        in_specs=[pl.BlockSpec((tm, D), row_map),
                  pl.BlockSpec((1, D, tf), col_w),
                  pl.BlockSpec((1, D, tf), col_w),
                  pl.BlockSpec((1, tf, D), row_w)],
        out_specs=pl.BlockSpec(memory_space=pl.ANY),
        scratch_shapes=[pltpu.VMEM((2, tm, D), F32), pltpu.SemaphoreType.DMA((2,))],
    )
    return pl.pallas_call(
        functools.partial(_expert_kernel, n_f=n_f),
        grid_spec=grid_spec,
        out_shape=jax.ShapeDtypeStruct((dump_base + dump_rows, D), F32),
        compiler_params=_params("arbitrary", "arbitrary"),
        name="expert_swiglu",
    )(tile_expert, tile_rows, tile_src, inv, xs, w_gate, w_up, w_down)


def _combine_kernel(h_ref, y1_ref, y2_ref, info_ref, p_ref, gp_ref, wpg_ref, wpp_ref, o_ref):
    info = info_ref[...]
    h = h_ref[...] + info[:, 2:3] * y1_ref[...] + info[:, 3:4] * y2_ref[...]
    o_ref[...] = _ple(h, p_ref[...], gp_ref[...], wpg_ref[...], wpp_ref[...])


def _combine(h, yk, info, p3, layer, g_ple, w_pg, w_pp):
    T, D = h.shape
    PD = p3.shape[2]
    tm = COMBINE_TILE
    nb = T // tm
    row = lambda w: pl.BlockSpec((tm, w), lambda i: (i, 0))
    return pl.pallas_call(
        _combine_kernel,
        grid=(nb,),
        in_specs=[row(D), row(D), pl.BlockSpec((tm, D), lambda i: (nb + i, 0)), row(LANES),
                  pl.BlockSpec((None, tm, PD), lambda i: (layer, i, 0)),
                  _const_spec((1, D)), _const_spec((D, D)), _const_spec((PD, D))],
        out_specs=row(D),
        out_shape=jax.ShapeDtypeStruct((T, D), F32),
        compiler_params=_params("parallel"),
        name="combine_ple",
    )(h, yk, yk, info, p3, g_ple, w_pg, w_pp)


def _rope_tables(S):
    inv = 1.0 / (ROPE_THETA ** (jnp.arange(0, DH, 2, dtype=F32) / DH))
    ang = jnp.arange(S, dtype=F32)[:, None] * inv[None, :]
    cos, sin = jnp.cos(ang), jnp.sin(ang)
    reps = HEAD_W // DH
    cos_t = jnp.tile(jnp.concatenate([cos, cos], axis=1), (1, reps))
    sin_t = jnp.tile(jnp.concatenate([-sin, sin], axis=1), (1, reps))
    return cos_t, sin_t


def _routing_tables(info, counts_f, n_experts, T):
    tm = EXPERT_TILE
    e12 = info[:, 0:2].astype(jnp.int32)
    rank = info[:, 4:6].astype(jnp.int32)
    counts = counts_f[0, :n_experts].astype(jnp.int32)
    padded = (counts + tm - 1) // tm * tm
    ends = jnp.cumsum(padded)
    starts = ends - padded
    onehot = e12[:, :, None] == jnp.arange(n_experts, dtype=jnp.int32)
    pos = jnp.sum(jnp.where(onehot, starts, 0), axis=-1) + rank
    P = T * TOP_K + n_experts * tm
    sub = EXPERT_SUB
    pad = jnp.stack([starts + counts, (counts + sub - 1) // sub * sub - counts])
    blk_start = jnp.arange(P // sub, dtype=jnp.int32) * sub
    holds_rows = jnp.any((blk_start[:, None] >= starts[None, :]) & (blk_start[:, None] < (starts + counts)[None, :]), axis=1)
    slot = jnp.where(holds_rows, -1, jnp.cumsum(~holds_rows) - 1)
    hit = slot[None, :] == jnp.arange(n_experts * tm // sub, dtype=jnp.int32)[:, None]
    empty_blocks = (jnp.sum(jnp.where(hit, jnp.arange(P // sub, dtype=jnp.int32) + 1, 0), axis=1) - 1).astype(jnp.int32)
    n_tiles = P // tm + 1
    tile_start = jnp.arange(n_tiles, dtype=jnp.int32) * tm
    tile_expert = jnp.minimum(jnp.sum(tile_start[:, None] >= ends[None, :], axis=1), n_experts - 1).astype(jnp.int32)
    tile_rows = jnp.clip((starts + counts)[tile_expert] - tile_start, 0, tm).astype(jnp.int32)
    tile_src = jnp.minimum(jnp.arange(n_tiles, dtype=jnp.int32), ends[-1] // tm - 1).astype(jnp.int32)
    return pos, pad, empty_blocks, P, tile_expert, tile_rows, tile_src


def kernel(x, p, norm_mix, norm_ffn, norm_ple, attn_w_qkv, attn_w_o, attn_q_norm, attn_k_norm,
           attn_lambda_q1, attn_lambda_k1, attn_lambda_q2, attn_lambda_k2, attn_subln,
           pool_w, pool_b, pool_scale, ffn_w_gate, ffn_w_up, ffn_w_down,
           moe_router, moe_w_gate, moe_w_up, moe_w_down, ple_w_proj, ple_w_gate):
    B, S, D = x.shape
    T = B * S
    n_experts = moe_router.shape[-1]
    x2 = x.reshape(T, D)
    p3 = p.reshape(p.shape[0], T, p.shape[-1])
    bf = lambda a: a.astype(BF16)
    vec = lambda a: a.reshape(1, -1)

    lam_init = 0.8 - 0.6 * math.exp(-0.3 * 0)
    cos_t, sin_t = _rope_tables(S)
    reps = HEAD_W // DH
    qk_gain = jnp.zeros((8, LANES), F32)
    qk_gain = qk_gain.at[0].set(jnp.tile(attn_q_norm[0], reps) * (ATTN_SCALE * math.log2(math.e)))
    qk_gain = qk_gain.at[1].set(jnp.tile(attn_k_norm[0], reps))
    group = jnp.arange(2 * LANES) // DH
    ones = (group[:, None] == group[None, :]).astype(BF16)
    qkv = _qkv(x2, vec(norm_mix[0]), attn_w_qkv[0], cos_t, sin_t, qk_gain, ones)
    lam_params = jnp.stack([attn_lambda_q1[0], attn_lambda_k1[0], attn_lambda_q2[0], attn_lambda_k2[0]])
    ao = _attention(qkv.reshape(B, S, 3 * D), lam_params, vec(attn_subln[0]), lam_init)
    h = _dense_tail(ao.reshape(T, D), x2, p3, 0, bf(attn_w_o[0]), vec(norm_ffn[0]),
                    bf(ffn_w_gate[0]), bf(ffn_w_up[0]), bf(ffn_w_down[0]),
                    vec(norm_ple[0]), bf(ple_w_gate[0]), bf(ple_w_proj[0]))

    w_router = jnp.zeros((D, LANES), F32).at[:, :n_experts].set(moe_router[0])
    w_router_hi = w_router.astype(BF16)
    w_router_pad = jnp.concatenate([w_router_hi, (w_router - w_router_hi.astype(F32)).astype(BF16)], axis=1)
    h, xn, info, counts = _pool_router(h, S, vec(norm_mix[1]), bf(pool_w[0]), vec(pool_b[0]),
                                       vec(pool_scale[0]), vec(norm_ffn[1]), w_router_pad, n_experts)
    pos, pad, empty_blocks, n_sorted, tile_expert, tile_rows, tile_src = _routing_tables(info, counts, n_experts, T)
    xs, inv = _row_scatter(xn, pos, pad, empty_blocks, n_sorted)
    yk = _experts(xs, inv, tile_expert, tile_rows, tile_src, moe_w_gate[0], moe_w_up[0], moe_w_down[0], T)
    out = _combine(h, yk, info, p3, 1, vec(norm_ple[1]), bf(ple_w_gate[1]), bf(ple_w_proj[1]))
    return out.reshape(B, S, D)
```
